```python
import math
import jax, jax.numpy as jnp
from jax import lax
import numpy as np

D_MODEL = 1024
BATCH = 8
SEQ = 2048
DEPTH = 1
DEC_BATCH = 128
DEC_SEQ = 1
PAST_LEN = 16384
PAGE_SIZE = 128

D_CONV = D_MODEL
CONV_A_WIDTH = 31
D_SSM = D_MODEL
SSM_HEAD_DIM = 64
SSM_HEADS = D_SSM // SSM_HEAD_DIM
SSM_GROUPS = 2
D_STATE = 128
SSM_CONV_WIDTH = 4
SSM_CHUNK = 128
D_XBC = D_SSM + 2 * SSM_GROUPS * D_STATE
D_MIX = D_CONV + D_SSM
D_IN = 2 * D_CONV + D_SSM + D_XBC + SSM_HEADS
D_FF = 2816
FFN_RES_WEIGHT = 0.5
NORM_EPS = 1e-5

kernel_name = "hybrid_conformerconv_ssd_macaron_step"


def rms_norm(x, g):
    xf = x.astype(jnp.float32)
    y = xf * lax.rsqrt(jnp.mean(xf * xf, axis=-1, keepdims=True) + NORM_EPS)
    return (y * g.astype(jnp.float32)).astype(x.dtype)


def layer_norm(x, g, b):
    xf = x.astype(jnp.float32)
    mu = jnp.mean(xf, axis=-1, keepdims=True)
    xc = xf - mu
    var = jnp.mean(xc * xc, axis=-1, keepdims=True)
    y = xc * lax.rsqrt(var + NORM_EPS) * g.astype(jnp.float32) + b.astype(jnp.float32)
    return y.astype(x.dtype)


def grouped_rms_norm(y, g, n_groups):
    b, l, c = y.shape
    yg = y.reshape(b, l, n_groups, c // n_groups)
    yg = yg * lax.rsqrt(jnp.mean(yg * yg, axis=-1, keepdims=True) + NORM_EPS)
    return yg.reshape(b, l, c) * g.astype(jnp.float32)


def swiglu_ffn(x, w_gate, w_up, w_down):
    return (jax.nn.silu(x @ w_gate) * (x @ w_up)) @ w_down


def causal_dwconv(u, buf, w, bias):
    k_minus_1 = buf.shape[1]
    ext = jnp.concatenate([buf.astype(u.dtype), u], axis=1)
    y = lax.conv_general_dilated(
        ext, w[:, None, :].astype(u.dtype), window_strides=(1,), padding="VALID",
        dimension_numbers=("NWC", "WIO", "NWC"), feature_group_count=u.shape[-1])
    new_buf = ext[:, ext.shape[1] - k_minus_1:]
    return y + bias.astype(u.dtype), new_buf


def ssd_scan(x, dt, A, Bm, Cm, h0):
    b, l, H, P = x.shape
    G, N = Bm.shape[2], Bm.shape[3]
    R = H // G
    Q = min(SSM_CHUNK, l)
    pad = (-l) % Q
    if pad:
        padf = lambda a: jnp.pad(a, [(0, 0), (0, pad)] + [(0, 0)] * (a.ndim - 2))
        x, dt, Bm, Cm = padf(x), padf(dt), padf(Bm), padf(Cm)
    c = (l + pad) // Q
    X = (x * dt[..., None]).reshape(b, c, Q, G, R, P)
    a = (dt * A).reshape(b, c, Q, G, R)
    a_cs = jnp.cumsum(a, axis=2)
    Bc = Bm.reshape(b, c, Q, G, N)
    Cc = Cm.reshape(b, c, Q, G, N)
    seg = a_cs[:, :, :, None] - a_cs[:, :, None, :]
    causal = jnp.tril(jnp.ones((Q, Q), dtype=bool))[None, None, :, :, None, None]
    Lmat = jnp.exp(jnp.where(causal, seg, -jnp.inf))
    CB = jnp.einsum("bcign,bcjgn->bcijg", Cc, Bc)
    y_diag = jnp.einsum("bcijg,bcijgr,bcjgrp->bcigrp", CB, Lmat, X)
    decay_to_end = jnp.exp(a_cs[:, :, -1:] - a_cs)
    s_local = jnp.einsum("bcjgn,bcjgr,bcjgrp->bcgrpn", Bc, decay_to_end, X)
    chunk_decay = jnp.exp(a_cs[:, :, -1])

    def step(h, inp):
        s, d = inp
        return h * d[..., None, None] + s, h

    h_final, h_prev = lax.scan(step, h0.reshape(b, G, R, P, N),
                               (jnp.swapaxes(s_local, 0, 1), jnp.swapaxes(chunk_decay, 0, 1)))
    h_prev = jnp.swapaxes(h_prev, 0, 1)
    y_off = jnp.einsum("bcign,bcgrpn,bcigr->bcigrp", Cc, h_prev, jnp.exp(a_cs))
    y = (y_diag + y_off).reshape(b, c * Q, H, P)[:, :l]
    return y, h_final.reshape(b, H, P, N)


def decoder_layer(x, conv_a_buf, conv_b_buf, ssm_state, p):
    (ffn1_norm, ffn1_w_gate, ffn1_w_up, ffn1_w_down, mix_norm, w_in,
     conv_dw_w, conv_dw_b, conv_ln_g, conv_ln_b,
     ssm_conv_w, ssm_conv_b, ssm_dt_bias, ssm_a_log, ssm_d, ssm_norm,
     w_out, ffn2_norm, ffn2_w_gate, ffn2_w_up, ffn2_w_down) = p
    x = x + FFN_RES_WEIGHT * swiglu_ffn(rms_norm(x, ffn1_norm), ffn1_w_gate, ffn1_w_up, ffn1_w_down)
    h = rms_norm(x, mix_norm)
    bsz, l, _ = h.shape
    proj = h @ w_in
    cuts = [D_CONV, 2 * D_CONV, 2 * D_CONV + D_SSM, 2 * D_CONV + D_SSM + D_XBC]
    glu_a, glu_b, z, xbc, dt_raw = jnp.split(proj, cuts, axis=-1)
    u = glu_a * jax.nn.sigmoid(glu_b)
    ua, new_conv_a = causal_dwconv(u, conv_a_buf, conv_dw_w, conv_dw_b)
    ya = jax.nn.silu(layer_norm(ua, conv_ln_g, conv_ln_b))
    xbc_c, new_conv_b = causal_dwconv(xbc, conv_b_buf, ssm_conv_w, ssm_conv_b)
    xbc_c = jax.nn.silu(xbc_c).astype(jnp.float32)
    xs, Bm, Cm = jnp.split(xbc_c, [D_SSM, D_SSM + SSM_GROUPS * D_STATE], axis=-1)
    dt = jax.nn.softplus(dt_raw.astype(jnp.float32) + ssm_dt_bias.astype(jnp.float32))
    A = -jnp.exp(ssm_a_log.astype(jnp.float32))
    xs_h = xs.reshape(bsz, l, SSM_HEADS, SSM_HEAD_DIM)
    ys, new_ssm = ssd_scan(xs_h, dt, A,
                           Bm.reshape(bsz, l, SSM_GROUPS, D_STATE),
                           Cm.reshape(bsz, l, SSM_GROUPS, D_STATE),
                           ssm_state.astype(jnp.float32))
    ys = ys + ssm_d.astype(jnp.float32)[:, None] * xs_h
    ys = ys.reshape(bsz, l, D_SSM) * jax.nn.silu(z.astype(jnp.float32))
    ys = grouped_rms_norm(ys, ssm_norm, SSM_GROUPS).astype(x.dtype)
    x = x + jnp.concatenate([ya, ys], axis=-1) @ w_out
    x = x + FFN_RES_WEIGHT * swiglu_ffn(rms_norm(x, ffn2_norm), ffn2_w_gate, ffn2_w_up, ffn2_w_down)
    return x, new_conv_a, new_conv_b, new_ssm.astype(ssm_state.dtype)


def setup_inputs(seed: int = 0) -> dict:
    key = jax.random.key(seed)
    ks = jax.random.split(key, 32)
    f32 = jnp.float32
    L = DEPTH

    def nrm(k, shape, scale):
        return scale * jax.random.normal(k, shape, f32)

    def gain(k, shape):
        return 1.0 + 0.02 * jax.random.normal(k, shape, f32)

    dt0 = jnp.exp(jax.random.uniform(ks[0], (L, SSM_HEADS), f32, math.log(1e-3), math.log(1e-1)))
    dt_bias = dt0 + jnp.log(-jnp.expm1(-dt0))
    return {
        "x_prompt": nrm(ks[1], (BATCH, SEQ, D_MODEL), 1.0),
        "x_sample": nrm(ks[2], (DEC_BATCH, DEC_SEQ, D_MODEL), 1.0),
        "state_conv_a": nrm(ks[3], (L, DEC_BATCH, CONV_A_WIDTH - 1, D_CONV), 0.5),
        "state_conv_b": nrm(ks[4], (L, DEC_BATCH, SSM_CONV_WIDTH - 1, D_XBC), 0.5),
        "state_ssm": nrm(ks[5], (L, DEC_BATCH, SSM_HEADS, SSM_HEAD_DIM, D_STATE), 0.1),
        "ffn1_norm": gain(ks[6], (L, D_MODEL)),
        "ffn1_w_gate": nrm(ks[7], (L, D_MODEL, D_FF), D_MODEL ** -0.5),
        "ffn1_w_up": nrm(ks[8], (L, D_MODEL, D_FF), D_MODEL ** -0.5),
        "ffn1_w_down": nrm(ks[9], (L, D_FF, D_MODEL), D_FF ** -0.5),
        "mix_norm": gain(ks[10], (L, D_MODEL)),
        "w_in": nrm(ks[11], (L, D_MODEL, D_IN), D_MODEL ** -0.5),
        "conv_dw_w": nrm(ks[12], (L, CONV_A_WIDTH, D_CONV), CONV_A_WIDTH ** -0.5),
        "conv_dw_b": nrm(ks[13], (L, D_CONV), 0.02),
        "conv_ln_g": gain(ks[14], (L, D_CONV)),
        "conv_ln_b": nrm(ks[15], (L, D_CONV), 0.02),
        "ssm_conv_w": nrm(ks[16], (L, SSM_CONV_WIDTH, D_XBC), SSM_CONV_WIDTH ** -0.5),
        "ssm_conv_b": nrm(ks[17], (L, D_XBC), 0.02),
        "ssm_dt_bias": dt_bias,
        "ssm_a_log": jnp.log(jax.random.uniform(ks[18], (L, SSM_HEADS), f32, 1.0, 16.0)),
        "ssm_d": 1.0 + 0.1 * jax.random.normal(ks[19], (L, SSM_HEADS), f32),
        "ssm_norm": gain(ks[20], (L, D_SSM)),
        "w_out": nrm(ks[21], (L, D_MIX, D_MODEL), D_MIX ** -0.5),
        "ffn2_norm": gain(ks[22], (L, D_MODEL)),
        "ffn2_w_gate": nrm(ks[23], (L, D_MODEL, D_FF), D_MODEL ** -0.5),
        "ffn2_w_up": nrm(ks[24], (L, D_MODEL, D_FF), D_MODEL ** -0.5),
        "ffn2_w_down": nrm(ks[25], (L, D_FF, D_MODEL), D_FF ** -0.5),
        "final_norm": gain(ks[26], (D_MODEL,)),
    }


def reference(x_prompt, x_sample, state_conv_a, state_conv_b, state_ssm,
              ffn1_norm, ffn1_w_gate, ffn1_w_up, ffn1_w_down, mix_norm, w_in,
              conv_dw_w, conv_dw_b, conv_ln_g, conv_ln_b,
              ssm_conv_w, ssm_conv_b, ssm_dt_bias, ssm_a_log, ssm_d, ssm_norm,
              w_out, ffn2_norm, ffn2_w_gate, ffn2_w_up, ffn2_w_down, final_norm):
    layer_ws = (ffn1_norm, ffn1_w_gate, ffn1_w_up, ffn1_w_down, mix_norm, w_in,
                conv_dw_w, conv_dw_b, conv_ln_g, conv_ln_b,
                ssm_conv_w, ssm_conv_b, ssm_dt_bias, ssm_a_log, ssm_d, ssm_norm,
                w_out, ffn2_norm, ffn2_w_gate, ffn2_w_up, ffn2_w_down)
    bp = x_prompt.shape[0]
    yp, ysm = x_prompt, x_sample
    pa, pb, ps, sa, sb, ss = [], [], [], [], [], []
    for i in range(DEPTH):
        p = tuple(w[i] for w in layer_ws)
        zeros_a = jnp.zeros((bp, CONV_A_WIDTH - 1, D_CONV), yp.dtype)
        zeros_b = jnp.zeros((bp, SSM_CONV_WIDTH - 1, D_XBC), yp.dtype)
        zeros_s = jnp.zeros((bp, SSM_HEADS, SSM_HEAD_DIM, D_STATE), state_ssm.dtype)
        yp, na, nb, ns = decoder_layer(yp, zeros_a, zeros_b, zeros_s, p)
        pa.append(na); pb.append(nb); ps.append(ns)
        ysm, na, nb, ns = decoder_layer(ysm, state_conv_a[i], state_conv_b[i], state_ssm[i], p)
        sa.append(na); sb.append(nb); ss.append(ns)
    y_prompt = rms_norm(yp, final_norm)
    y_sample = rms_norm(ysm, final_norm)
    return (y_prompt, y_sample,
            jnp.stack(pa), jnp.stack(pb), jnp.stack(ps),
            jnp.stack(sa), jnp.stack(sb), jnp.stack(ss))
```

```python
import functools

import jax
import jax.numpy as jnp
from jax import lax
from jax.experimental import pallas as pl
from jax.experimental.pallas import tpu as pltpu

D_MODEL = 1024
D_FF = 2816
D_CONV = 1024
CONV_A_WIDTH = 31
D_SSM = 1024
SSM_HEAD_DIM = 64
SSM_HEADS = 16
SSM_GROUPS = 2
HEADS_PER_GROUP = SSM_HEADS // SSM_GROUPS
D_STATE = 128
SSM_CONV_WIDTH = 4
CHUNK = 128
D_XBC = D_SSM + 2 * SSM_GROUPS * D_STATE
D_BC = SSM_GROUPS * D_STATE
FFN_RES_WEIGHT = 0.5
NORM_EPS = 1e-5

LANES = 128
SUBLANES = 8
DT_PAD = LANES
A_TAIL = 32
B_TAIL = 8
VMEM_LIMIT = 56 * 1024 * 1024
NEG_BIG = -1e30

ACT_DTYPE = jnp.float32
MXU_DTYPE = jnp.bfloat16


def _dot(a, b):
    return jnp.dot(a.astype(MXU_DTYPE), b.astype(MXU_DTYPE), preferred_element_type=jnp.float32)


def _dot_nt(a, b):
    return lax.dot_general(a.astype(MXU_DTYPE), b.astype(MXU_DTYPE), (((1,), (1,)), ((), ())),
                           preferred_element_type=jnp.float32)


def _split2(v):
    hi = v.astype(MXU_DTYPE)
    lo = (v - hi.astype(jnp.float32)).astype(MXU_DTYPE)
    return hi, lo


def _dot_split_lhs(v, m):
    hi, lo = _split2(v)
    return (jnp.dot(hi, m, preferred_element_type=jnp.float32)
            + jnp.dot(lo, m, preferred_element_type=jnp.float32))


def _dot_split_rhs(m, v):
    hi = v.astype(MXU_DTYPE)
    r1 = v - hi.astype(jnp.float32)
    mid = r1.astype(MXU_DTYPE)
    lo = (r1 - mid.astype(jnp.float32)).astype(MXU_DTYPE)
    return (jnp.dot(m, hi, preferred_element_type=jnp.float32)
            + jnp.dot(m, mid, preferred_element_type=jnp.float32)
            + jnp.dot(m, lo, preferred_element_type=jnp.float32))


def _rms_norm(x, g):
    return x * lax.rsqrt(jnp.mean(x * x, axis=-1, keepdims=True) + NORM_EPS) * g


def _silu(x):
    return x * jax.nn.sigmoid(x)


def _softplus(x):
    return jnp.maximum(x, 0.0) + jnp.log1p(jnp.exp(-jnp.abs(x)))


def _layer_norm_silu(x, g, b):
    mu = jnp.mean(x, axis=-1, keepdims=True)
    xc = x - mu
    var = jnp.mean(xc * xc, axis=-1, keepdims=True)
    return _silu(xc * lax.rsqrt(var + NORM_EPS) * g + b)


def _gate_group_norm(y, z, g):
    y = y * _silu(z)
    w = D_SSM // SSM_GROUPS
    parts = []
    for i in range(SSM_GROUPS):
        yg = y[:, i * w:(i + 1) * w]
        parts.append(yg * lax.rsqrt(jnp.mean(yg * yg, axis=-1, keepdims=True) + NORM_EPS))
    return jnp.concatenate(parts, axis=-1) * g


def _const_spec(shape):
    nd = len(shape)
    return pl.BlockSpec(shape, lambda *_: (0,) * nd, pipeline_mode=pl.Buffered(1))


def _full_spec(shape):
    nd = len(shape)
    return pl.BlockSpec(shape, lambda *_: (0,) * nd)


def _ffn_kernel(*refs, with_outproj, with_final_norm):
    refs = list(refs)
    x_ref = refs.pop(0)
    if with_outproj:
        mix_ref = refs.pop(0)
        wo_ref = refs.pop(0)
    g_ref, wg_ref, wu_ref, wd_ref = refs[:4]
    refs = refs[4:]
    if with_final_norm:
        fg_ref = refs.pop(0)
    o_ref = refs.pop(0)

    x = x_ref[...]
    if with_outproj:
        x = x + _dot(mix_ref[...], wo_ref[...])
    xn = _rms_norm(x, g_ref[...]).astype(MXU_DTYPE)
    gate = jnp.dot(xn, wg_ref[...], preferred_element_type=jnp.float32)
    up = jnp.dot(xn, wu_ref[...], preferred_element_type=jnp.float32)
    h = (_silu(gate) * up).astype(MXU_DTYPE)
    y = x + FFN_RES_WEIGHT * jnp.dot(h, wd_ref[...], preferred_element_type=jnp.float32)
    if with_final_norm:
        y = _rms_norm(y, fg_ref[...])
    o_ref[...] = y


def _row_tile(m):
    return min(m, 512)


def _ffn(x, norm_g, wg, wu, wd, mix=None, wo=None, final_g=None):
    m = x.shape[0]
    tm = _row_tile(m)
    with_outproj = mix is not None
    with_final_norm = final_g is not None
    row = lambda i: (i, 0)
    args, specs = [x], [pl.BlockSpec((tm, D_MODEL), row)]
    if with_outproj:
        args += [mix, wo]
        specs += [pl.BlockSpec((tm, mix.shape[1]), row), _const_spec(wo.shape)]
    args += [norm_g, wg, wu, wd]
    specs += [_const_spec(norm_g.shape), _const_spec(wg.shape), _const_spec(wu.shape), _const_spec(wd.shape)]
    if with_final_norm:
        args.append(final_g)
        specs.append(_const_spec(final_g.shape))
    return pl.pallas_call(
        functools.partial(_ffn_kernel, with_outproj=with_outproj, with_final_norm=with_final_norm),
        grid=(m // tm,),
        in_specs=specs,
        out_specs=pl.BlockSpec((tm, D_MODEL), row),
        out_shape=jax.ShapeDtypeStruct((m, D_MODEL), jnp.float32),
        compiler_params=pltpu.CompilerParams(dimension_semantics=("arbitrary",), vmem_limit_bytes=VMEM_LIMIT),
        name="ffn_out" if with_outproj else "ffn_in",
    )(*args)


def _inproj_kernel(x_ref, g_ref, w_ref, u_ref, z_ref, xbc_ref, dt_ref):
    xn = _rms_norm(x_ref[...], g_ref[...]).astype(MXU_DTYPE)

    def proj(lo, width):
        return jnp.dot(xn, w_ref[:, lo:lo + width], preferred_element_type=jnp.float32)

    glu_a = proj(0, D_CONV)
    glu_b = proj(D_CONV, D_CONV)
    u_ref[...] = (glu_a * jax.nn.sigmoid(glu_b)).astype(u_ref.dtype)
    z_ref[...] = proj(2 * D_CONV, D_SSM).astype(z_ref.dtype)
    xbc_ref[...] = proj(2 * D_CONV + D_SSM, D_XBC).astype(xbc_ref.dtype)
    dt_ref[...] = proj(2 * D_CONV + D_SSM + D_XBC, DT_PAD)


def _inproj(x, norm_g, w_in_p):
    m = x.shape[0]
    tm = _row_tile(m)
    row = lambda i: (i, 0)
    widths = (D_CONV, D_SSM, D_XBC, DT_PAD)
    dtypes = (ACT_DTYPE, ACT_DTYPE, ACT_DTYPE, jnp.float32)
    return pl.pallas_call(
        _inproj_kernel,
        grid=(m // tm,),
        in_specs=[pl.BlockSpec((tm, D_MODEL), row), _const_spec(norm_g.shape), _const_spec(w_in_p.shape)],
        out_specs=[pl.BlockSpec((tm, w), row) for w in widths],
        out_shape=[jax.ShapeDtypeStruct((m, w), d) for w, d in zip(widths, dtypes)],
        compiler_params=pltpu.CompilerParams(dimension_semantics=("arbitrary",), vmem_limit_bytes=VMEM_LIMIT),
        name="inproj",
    )(x, norm_g, w_in_p)


def _mixer_kernel(u_ref, z_ref, xbc_ref, dt_ref,
                  cwa_ref, cba_ref, lng_ref, lnb_ref,
                  cwb_ref, cbb_ref, dtb_ref, a_ref, dexp_ref, nrm_ref, e_ref,
                  mix_ref, ssm_ref,
                  exta_ref, extb_ref, ht_ref, ua_ref, xc_ref, y_ref):
    c = pl.program_id(1)
    f32 = jnp.float32

    @pl.when(c == 0)
    def _():
        exta_ref[0:A_TAIL, :] = jnp.zeros((A_TAIL, D_CONV), f32)
        extb_ref[0:B_TAIL, :] = jnp.zeros((B_TAIL, D_XBC), f32)
        ht_ref[...] = jnp.zeros_like(ht_ref)

    exta_ref[A_TAIL:A_TAIL + CHUNK, :] = u_ref[0].astype(f32)
    first_a = A_TAIL - (CONV_A_WIDTH - 1)
    for j in range(D_CONV // LANES):
        lanes = slice(j * LANES, (j + 1) * LANES)
        acc = jnp.zeros((CHUNK, LANES), f32)
        for k in range(CONV_A_WIDTH):
            acc = acc + exta_ref[first_a + k:first_a + k + CHUNK, lanes] * cwa_ref[k:k + 1, lanes]
        ua_ref[:, lanes] = acc + cba_ref[:, lanes]
    exta_ref[0:A_TAIL, :] = exta_ref[CHUNK:CHUNK + A_TAIL, :]
    ya = _layer_norm_silu(ua_ref[...], lng_ref[...], lnb_ref[...])
    mix_ref[0, :, 0:D_CONV] = ya.astype(mix_ref.dtype)

    extb_ref[B_TAIL:B_TAIL + CHUNK, :] = xbc_ref[0].astype(f32)
    first_b = B_TAIL - (SSM_CONV_WIDTH - 1)
    for j in range(D_XBC // LANES):
        lanes = slice(j * LANES, (j + 1) * LANES)
        acc = jnp.zeros((CHUNK, LANES), f32)
        for k in range(SSM_CONV_WIDTH):
            acc = acc + extb_ref[first_b + k:first_b + k + CHUNK, lanes] * cwb_ref[k:k + 1, lanes]
        xc_ref[:, lanes] = _silu(acc + cbb_ref[:, lanes])
    extb_ref[0:B_TAIL, :] = extb_ref[CHUNK:CHUNK + B_TAIL, :]

    dt = _softplus(dt_ref[0] + dtb_ref[...])
    a = dt * a_ref[...]
    rows = lax.broadcasted_iota(jnp.int32, (CHUNK, CHUNK), 0)
    cols = lax.broadcasted_iota(jnp.int32, (CHUNK, CHUNK), 1)
    causal = rows >= cols
    tri = causal.astype(MXU_DTYPE)
    a_cs = _dot_split_rhs(tri, a)
    a_cs_t = a_cs.T
    dt_t = dt.T
    ea = jnp.exp(a_cs)
    w_end = jnp.exp(a_cs[CHUNK - 1:CHUNK, :] - a_cs) * dt
    e_mat = e_ref[...]
    w_exp = _dot_split_lhs(w_end, e_mat)
    ea_exp = _dot_split_lhs(ea, e_mat)
    xs = xc_ref[:, 0:D_SSM]
    xs_b = xs.astype(MXU_DTYPE)
    xw_b = (xs * w_exp).astype(MXU_DTYPE)
    lane = lax.broadcasted_iota(jnp.int32, (CHUNK, LANES), 1)
    gw = HEADS_PER_GROUP * SSM_HEAD_DIM
    for g in range(SSM_GROUPS):
        b_g = xc_ref[:, D_SSM + g * D_STATE:D_SSM + (g + 1) * D_STATE]
        c_g = xc_ref[:, D_SSM + D_BC + g * D_STATE:D_SSM + D_BC + (g + 1) * D_STATE]
        c_gb = c_g.astype(MXU_DTYPE)
        cb = _dot_nt(c_gb, b_g)
        ht_g = ht_ref[:, g * gw:(g + 1) * gw]
        y_off = _dot(c_gb, ht_g) * ea_exp[:, g * gw:(g + 1) * gw]
        for hp in range(HEADS_PER_GROUP // 2):
            res = []
            for e in range(2):
                h = g * HEADS_PER_GROUP + 2 * hp + e
                seg = a_cs[:, h:h + 1] - a_cs_t[h:h + 1, :]
                l_mat = jnp.exp(jnp.where(causal, seg, NEG_BIG))
                gmat = (cb * l_mat * dt_t[h:h + 1, :]).astype(MXU_DTYPE)
                lo = g * gw + hp * LANES
                res.append(jnp.dot(gmat, xs_b[:, lo:lo + LANES], preferred_element_type=f32))
            y_pair = jnp.where(lane < SSM_HEAD_DIM, res[0], res[1])
            y_ref[:, lo:lo + LANES] = y_pair + y_off[:, hp * LANES:(hp + 1) * LANES]
        s_loc = _dot(b_g.T, xw_b[:, g * gw:(g + 1) * gw])
        ht_ref[:, g * gw:(g + 1) * gw] = ht_g * ea_exp[CHUNK - 1:CHUNK, g * gw:(g + 1) * gw] + s_loc

    y = y_ref[...] + dexp_ref[...] * xs
    ys = _gate_group_norm(y, z_ref[0].astype(f32), nrm_ref[...])
    mix_ref[0, :, D_CONV:D_CONV + D_SSM] = ys.astype(mix_ref.dtype)

    @pl.when(c == pl.num_programs(1) - 1)
    def _():
        for j in range(D_SSM // LANES):
            ssm_ref[0, j * LANES:(j + 1) * LANES, :] = ht_ref[:, j * LANES:(j + 1) * LANES].T


def _mixer(u, z, xbc, dt_raw, p):
    b, l, _ = u.shape
    nc = l // CHUNK
    tok = lambda i, c: (i, c, 0)
    consts = [p["cwa"], p["cba"], p["lng"], p["lnb"], p["cwb"], p["cbb"], p["dtb"], p["a"], p["dexp"], p["nrm"], p["e"]]
    f32 = jnp.float32
    return pl.pallas_call(
        _mixer_kernel,
        grid=(b, nc),
        in_specs=[pl.BlockSpec((1, CHUNK, D_CONV), tok), pl.BlockSpec((1, CHUNK, D_SSM), tok),
                  pl.BlockSpec((1, CHUNK, D_XBC), tok), pl.BlockSpec((1, CHUNK, DT_PAD), tok)]
                 + [_const_spec(a.shape) for a in consts],
        out_specs=[pl.BlockSpec((1, CHUNK, D_CONV + D_SSM), tok),
                   pl.BlockSpec((1, D_SSM, D_STATE), lambda i, c: (i, 0, 0))],
        out_shape=[jax.ShapeDtypeStruct((b, l, D_CONV + D_SSM), ACT_DTYPE),
                   jax.ShapeDtypeStruct((b, D_SSM, D_STATE), f32)],
        scratch_shapes=[pltpu.VMEM((A_TAIL + CHUNK, D_CONV), f32),
                        pltpu.VMEM((B_TAIL + CHUNK, D_XBC), f32),
                        pltpu.VMEM((D_STATE, D_SSM), f32),
                        pltpu.VMEM((CHUNK, D_CONV), f32),
                        pltpu.VMEM((CHUNK, D_XBC), f32),
                        pltpu.VMEM((CHUNK, D_SSM), f32)],
        compiler_params=pltpu.CompilerParams(dimension_semantics=("arbitrary", "arbitrary"),
                                             vmem_limit_bytes=VMEM_LIMIT),
        name="mixer",
    )(u, z, xbc, dt_raw, *consts)


def _sample_conv_a_kernel(st_ref, u_ref, w_ref, cba_ref, lng_ref, lnb_ref, new_ref, ya_ref, acc_ref):
    k = pl.program_id(0)
    hist = CONV_A_WIDTH - 1

    @pl.when(k == 0)
    def _():
        acc_ref[...] = jnp.zeros_like(acc_ref)

    @pl.when(k < hist)
    def _():
        acc_ref[...] += st_ref[...] * w_ref[...]

    @pl.when(jnp.logical_and(k >= 1, k < hist))
    def _():
        new_ref[...] = st_ref[...]

    @pl.when(k == hist)
    def _():
        u = u_ref[...].astype(jnp.float32)
        new_ref[...] = u
        ua = acc_ref[...] + u * w_ref[...] + cba_ref[...]
        ya_ref[...] = _layer_norm_silu(ua, lng_ref[...], lnb_ref[...])


def _sample_conv_a(state, u, p):
    s = state.shape[0]
    hist = CONV_A_WIDTH - 1
    st2 = state.reshape(s, hist * D_CONV)
    w3 = p["cwa"][:CONV_A_WIDTH].reshape(CONV_A_WIDTH, 1, D_CONV)
    new, ya = pl.pallas_call(
        _sample_conv_a_kernel,
        grid=(CONV_A_WIDTH,),
        in_specs=[pl.BlockSpec((s, D_CONV), lambda k: (0, jnp.minimum(k, hist - 1))),
                  _const_spec(u.shape),
                  pl.BlockSpec((None, 1, D_CONV), lambda k: (k, 0, 0)),
                  _const_spec(p["cba"].shape), _const_spec(p["lng"].shape), _const_spec(p["lnb"].shape)],
        out_specs=[pl.BlockSpec((s, D_CONV), lambda k: (0, jnp.maximum(k - 1, 0))),
                   pl.BlockSpec((s, D_CONV), lambda k: (0, 0))],
        out_shape=[jax.ShapeDtypeStruct((s, hist * D_CONV), jnp.float32),
                   jax.ShapeDtypeStruct((s, D_CONV), jnp.float32)],
        scratch_shapes=[pltpu.VMEM((s, D_CONV), jnp.float32)],
        compiler_params=pltpu.CompilerParams(dimension_semantics=("arbitrary",), vmem_limit_bytes=VMEM_LIMIT),
        name="sample_conv_a",
    )(st2, u, w3, p["cba"], p["lng"], p["lnb"])
    return new.reshape(s, hist, D_CONV), ya


def _sample_conv_b_kernel(st_ref, xbc_ref, dt_ref, cwb_ref, cbb_ref, dtb_ref, a_ref, e_ref,
                          new_ref, xs_ref, x_ref, da_ref, bc_ref):
    hist = SSM_CONV_WIDTH - 1
    xbc = xbc_ref[...].astype(jnp.float32)
    acc = xbc * cwb_ref[hist:hist + 1, :] + cbb_ref[...]
    for k in range(hist):
        acc = acc + st_ref[:, k * D_XBC:(k + 1) * D_XBC] * cwb_ref[k:k + 1, :]
    for k in range(hist - 1):
        new_ref[:, k * D_XBC:(k + 1) * D_XBC] = st_ref[:, (k + 1) * D_XBC:(k + 2) * D_XBC]
    new_ref[:, (hist - 1) * D_XBC:hist * D_XBC] = xbc
    xc = _silu(acc)
    xs = xc[:, 0:D_SSM]
    dt = _softplus(dt_ref[...] + dtb_ref[...])
    da = jnp.exp(dt * a_ref[...])
    e_mat = e_ref[...]
    xs_ref[...] = xs
    x_ref[...] = xs * _dot_split_lhs(dt, e_mat)
    da_ref[...] = _dot_split_lhs(da, e_mat)
    bc_ref[...] = xc[:, D_SSM:D_XBC]


def _sample_conv_b(state, xbc, dt_raw, p):
    s = state.shape[0]
    hist = SSM_CONV_WIDTH - 1
    st2 = state.reshape(s, hist * D_XBC)
    f32 = jnp.float32
    args = [st2, xbc, dt_raw, p["cwb"], p["cbb"], p["dtb"], p["a"], p["e"]]
    outs = pl.pallas_call(
        _sample_conv_b_kernel,
        grid=(1,),
        in_specs=[_const_spec(a.shape) for a in args],
        out_specs=[_full_spec(sh) for sh in ((s, hist * D_XBC), (s, D_SSM), (s, D_SSM), (s, D_SSM), (s, 2 * D_BC))],
        out_shape=[jax.ShapeDtypeStruct((s, hist * D_XBC), f32), jax.ShapeDtypeStruct((s, D_SSM), f32),
                   jax.ShapeDtypeStruct((s, D_SSM), f32), jax.ShapeDtypeStruct((s, D_SSM), f32),
                   jax.ShapeDtypeStruct((s, 2 * D_BC), f32)],
        compiler_params=pltpu.CompilerParams(dimension_semantics=("arbitrary",), vmem_limit_bytes=VMEM_LIMIT),
        name="sample_conv_b",
    )(*args)
    new, xs, x_dt, da_exp, bc = outs
    return new.reshape(s, hist, D_XBC), xs, x_dt, da_exp, bc


def _sample_ssd_kernel(st_ref, x_ref, da_ref, b_ref, c_ref, new_ref, y_ref):
    j = pl.program_id(1)
    b_v = b_ref[...]
    c_v = c_ref[...]
    lane = lax.broadcasted_iota(jnp.int32, y_ref.shape, 1)

    def one_head(e):
        base = e * SSM_HEAD_DIM
        da_col = da_ref[:, base:base + 1]
        y_acc = jnp.zeros(y_ref.shape, jnp.float32)
        for pp in range(SSM_HEAD_DIM):
            cols = slice(pp * D_STATE, (pp + 1) * D_STATE)
            h_new = st_ref[:, cols] * da_col + x_ref[:, base + pp:base + pp + 1] * b_v
            new_ref[:, cols] = h_new
            y_col = jnp.sum(h_new * c_v, axis=1, keepdims=True)
            y_acc = jnp.where(lane == base + pp, y_col, y_acc)
        return y_acc

    @pl.when(j == 0)
    def _():
        y_ref[...] = one_head(0)

    @pl.when(j == 1)
    def _():
        y_ref[...] = jnp.where(lane >= SSM_HEAD_DIM, one_head(1), y_ref[...])


def _sample_ssd(state, x_dt, da_exp, bc):
    s = state.shape[0]
    hw = SSM_HEAD_DIM * D_STATE
    st2 = state.reshape(s, SSM_HEADS * hw)
    pairs_per_group = HEADS_PER_GROUP // 2
    pair = lambda i, j: (0, i)
    new, y = pl.pallas_call(
        _sample_ssd_kernel,
        grid=(SSM_HEADS // 2, 2),
        in_specs=[pl.BlockSpec((s, hw), lambda i, j: (0, 2 * i + j)),
                  pl.BlockSpec((s, LANES), pair), pl.BlockSpec((s, LANES), pair),
                  pl.BlockSpec((s, D_STATE), lambda i, j: (0, i // pairs_per_group)),
                  pl.BlockSpec((s, D_STATE), lambda i, j: (0, SSM_GROUPS + i // pairs_per_group))],
        out_specs=[pl.BlockSpec((s, hw), lambda i, j: (0, 2 * i + j)), pl.BlockSpec((s, LANES), pair)],
        out_shape=[jax.ShapeDtypeStruct((s, SSM_HEADS * hw), jnp.float32),
                   jax.ShapeDtypeStruct((s, D_SSM), jnp.float32)],
        compiler_params=pltpu.CompilerParams(dimension_semantics=("arbitrary", "arbitrary"),
                                             vmem_limit_bytes=VMEM_LIMIT),
        name="sample_ssd",
    )(st2, x_dt, da_exp, bc, bc)
    return new.reshape(s, SSM_HEADS, SSM_HEAD_DIM, D_STATE), y


def _sample_gate_kernel(ya_ref, y_ref, xs_ref, z_ref, dexp_ref, nrm_ref, mix_ref):
    y = y_ref[...] + dexp_ref[...] * xs_ref[...]
    ys = _gate_group_norm(y, z_ref[...].astype(jnp.float32), nrm_ref[...])
    mix_ref[:, 0:D_CONV] = ya_ref[...].astype(mix_ref.dtype)
    mix_ref[:, D_CONV:D_CONV + D_SSM] = ys.astype(mix_ref.dtype)


def _sample_gate(ya, y, xs, z, p):
    s = ya.shape[0]
    args = [ya, y, xs, z, p["dexp"], p["nrm"]]
    return pl.pallas_call(
        _sample_gate_kernel,
        grid=(1,),
        in_specs=[_const_spec(a.shape) for a in args],
        out_specs=_full_spec((s, D_CONV + D_SSM)),
        out_shape=jax.ShapeDtypeStruct((s, D_CONV + D_SSM), ACT_DTYPE),
        compiler_params=pltpu.CompilerParams(dimension_semantics=("arbitrary",), vmem_limit_bytes=VMEM_LIMIT),
        name="sample_gate",
    )(*args)


def _prep_params(i, conv_dw_w, conv_dw_b, conv_ln_g, conv_ln_b, ssm_conv_w, ssm_conv_b, ssm_dt_bias, ssm_a_log,
                 ssm_d, ssm_norm):
    f32 = jnp.float32
    row = lambda v: v.reshape(1, -1).astype(f32)
    pad_heads = lambda v: jnp.pad(v.astype(f32), (0, DT_PAD - SSM_HEADS)).reshape(1, DT_PAD)
    head_of_channel = jnp.arange(D_SSM, dtype=jnp.int32) // SSM_HEAD_DIM
    expand = (jnp.arange(DT_PAD, dtype=jnp.int32)[:, None] == head_of_channel[None, :]).astype(MXU_DTYPE)
    return {
        "cwa": jnp.pad(conv_dw_w[i].astype(f32), ((0, A_TAIL - CONV_A_WIDTH), (0, 0))),
        "cba": row(conv_dw_b[i]), "lng": row(conv_ln_g[i]), "lnb": row(conv_ln_b[i]),
        "cwb": jnp.pad(ssm_conv_w[i].astype(f32), ((0, B_TAIL - SSM_CONV_WIDTH), (0, 0))),
        "cbb": row(ssm_conv_b[i]),
        "dtb": pad_heads(ssm_dt_bias[i]),
        "a": pad_heads(-jnp.exp(ssm_a_log[i].astype(f32))),
        "dexp": jnp.repeat(ssm_d[i].astype(f32), SSM_HEAD_DIM).reshape(1, D_SSM),
        "nrm": row(ssm_norm[i]),
        "e": expand,
    }


def kernel(x_prompt, x_sample, state_conv_a, state_conv_b, state_ssm, ffn1_norm, ffn1_w_gate, ffn1_w_up, ffn1_w_down, mix_norm, w_in, conv_dw_w, conv_dw_b, conv_ln_g, conv_ln_b, ssm_conv_w, ssm_conv_b, ssm_dt_bias, ssm_a_log, ssm_d, ssm_norm, w_out, ffn2_norm, ffn2_w_gate, ffn2_w_up, ffn2_w_down, final_norm):
    depth = ffn1_norm.shape[0]
    bp, lp, _ = x_prompt.shape
    bs = x_sample.shape[0]
    row = lambda v: v.reshape(1, -1).astype(jnp.float32)
    w16 = lambda w: w.astype(MXU_DTYPE)

    yp = x_prompt.reshape(bp * lp, D_MODEL)
    ysm = x_sample.reshape(bs, D_MODEL)
    outs = [[] for _ in range(6)]
    for i in range(depth):
        p = _prep_params(i, conv_dw_w, conv_dw_b, conv_ln_g, conv_ln_b, ssm_conv_w, ssm_conv_b, ssm_dt_bias,
                         ssm_a_log, ssm_d, ssm_norm)
        w_in_p = w16(jnp.pad(w_in[i], ((0, 0), (0, DT_PAD - SSM_HEADS))))
        f1 = (row(ffn1_norm[i]), w16(ffn1_w_gate[i]), w16(ffn1_w_up[i]), w16(ffn1_w_down[i]))
        f2 = (row(ffn2_norm[i]), w16(ffn2_w_gate[i]), w16(ffn2_w_up[i]), w16(ffn2_w_down[i]))
        wo = w16(w_out[i])
        fin = row(final_norm) if i == depth - 1 else None

        x1 = _ffn(yp, *f1)
        u, z, xbc, dt_raw = _inproj(x1, row(mix_norm[i]), w_in_p)
        mix, ssm_t = _mixer(u.reshape(bp, lp, D_CONV), z.reshape(bp, lp, D_SSM), xbc.reshape(bp, lp, D_XBC),
                            dt_raw.reshape(bp, lp, DT_PAD), p)
        yp = _ffn(x1, *f2, mix=mix.reshape(bp * lp, D_CONV + D_SSM), wo=wo, final_g=fin)
        outs[0].append(u.reshape(bp, lp, D_CONV)[:, lp - (CONV_A_WIDTH - 1):, :].astype(x_prompt.dtype))
        outs[1].append(xbc.reshape(bp, lp, D_XBC)[:, lp - (SSM_CONV_WIDTH - 1):, :].astype(x_prompt.dtype))
        outs[2].append(ssm_t.reshape(bp, SSM_HEADS, SSM_HEAD_DIM, D_STATE).astype(state_ssm.dtype))

        s1 = _ffn(ysm, *f1)
        su, sz, sxbc, sdt = _inproj(s1, row(mix_norm[i]), w_in_p)
        new_a, ya = _sample_conv_a(state_conv_a[i], su, p)
        new_b, xs, x_dt, da_exp, bc = _sample_conv_b(state_conv_b[i], sxbc, sdt, p)
        new_s, y = _sample_ssd(state_ssm[i].astype(jnp.float32), x_dt, da_exp, bc)
        smix = _sample_gate(ya, y, xs, sz, p)
        ysm = _ffn(s1, *f2, mix=smix, wo=wo, final_g=fin)
        outs[3].append(new_a.astype(x_sample.dtype))
        outs[4].append(new_b.astype(x_sample.dtype))
        outs[5].append(new_s.astype(state_ssm.dtype))

    return (yp.reshape(bp, lp, D_MODEL), ysm.reshape(bs, 1, D_MODEL)) + tuple(jnp.stack(o) for o in outs)
```

```python
import functools

import jax
import jax.numpy as jnp
from jax import lax
from jax.experimental import pallas as pl
from jax.experimental.pallas import tpu as pltpu

D_MODEL = 1024
D_FF = 2816
D_CONV = 1024
CONV_A_WIDTH = 31
D_SSM = 1024
SSM_HEAD_DIM = 64
SSM_HEADS = 16
SSM_GROUPS = 2
HEADS_PER_GROUP = SSM_HEADS // SSM_GROUPS
D_STATE = 128
SSM_CONV_WIDTH = 4
CHUNK = 128
D_XBC = D_SSM + 2 * SSM_GROUPS * D_STATE
D_BC = SSM_GROUPS * D_STATE
FFN_RES_WEIGHT = 0.5
NORM_EPS = 1e-5

LANES = 128
SUBLANES = 8
DT_PAD = LANES
A_TAIL = 32
B_TAIL = 8
VMEM_LIMIT = 56 * 1024 * 1024
NEG_BIG = -1e30

ACT_DTYPE = jnp.float32
MXU_DTYPE = jnp.bfloat16


def _dot(a, b):
    return jnp.dot(a.astype(MXU_DTYPE), b.astype(MXU_DTYPE), preferred_element_type=jnp.float32)


def _dot_nt(a, b):
    return lax.dot_general(a.astype(MXU_DTYPE), b.astype(MXU_DTYPE), (((1,), (1,)), ((), ())),
                           preferred_element_type=jnp.float32)


def _split2(v):
    hi = v.astype(MXU_DTYPE)
    lo = (v - hi.astype(jnp.float32)).astype(MXU_DTYPE)
    return hi, lo


def _dot_split_lhs(v, m):
    hi, lo = _split2(v)
    return (jnp.dot(hi, m, preferred_element_type=jnp.float32)
            + jnp.dot(lo, m, preferred_element_type=jnp.float32))


def _dot_split_rhs(m, v):
    hi = v.astype(MXU_DTYPE)
    r1 = v - hi.astype(jnp.float32)
    mid = r1.astype(MXU_DTYPE)
    lo = (r1 - mid.astype(jnp.float32)).astype(MXU_DTYPE)
    return (jnp.dot(m, hi, preferred_element_type=jnp.float32)
            + jnp.dot(m, mid, preferred_element_type=jnp.float32)
            + jnp.dot(m, lo, preferred_element_type=jnp.float32))


def _rms_norm(x, g):
    return x * lax.rsqrt(jnp.mean(x * x, axis=-1, keepdims=True) + NORM_EPS) * g


def _silu(x):
    return x * jax.nn.sigmoid(x)


def _softplus(x):
    return jnp.maximum(x, 0.0) + jnp.log1p(jnp.exp(-jnp.abs(x)))


def _layer_norm_silu(x, g, b):
    mu = jnp.mean(x, axis=-1, keepdims=True)
    xc = x - mu
    var = jnp.mean(xc * xc, axis=-1, keepdims=True)
    return _silu(xc * lax.rsqrt(var + NORM_EPS) * g + b)


def _gate_group_norm(y, z, g):
    y = y * _silu(z)
    w = D_SSM // SSM_GROUPS
    parts = []
    for i in range(SSM_GROUPS):
        yg = y[:, i * w:(i + 1) * w]
        parts.append(yg * lax.rsqrt(jnp.mean(yg * yg, axis=-1, keepdims=True) + NORM_EPS))
    return jnp.concatenate(parts, axis=-1) * g


def _causal_conv_tile(ext_ref, w_ref, lanes, tail, width):
    x = ext_ref[:, lanes]
    n_rows = x.shape[0]
    first = tail - (width - 1)
    acc = jnp.zeros((CHUNK, LANES), jnp.float32)
    for r in range(SUBLANES):
        taps = [k for k in range(width) if (first + k) % SUBLANES == r]
        if not taps:
            continue
        xr = x if r == 0 else pltpu.roll(x, n_rows - r, 0)
        for k in taps:
            base = first + k - r
            acc = acc + xr[base:base + CHUNK, :] * w_ref[k:k + 1, lanes]
    return acc


def _const_spec(shape):
    nd = len(shape)
    return pl.BlockSpec(shape, lambda *_: (0,) * nd, pipeline_mode=pl.Buffered(1))


def _full_spec(shape):
    nd = len(shape)
    return pl.BlockSpec(shape, lambda *_: (0,) * nd)


def _ffn_kernel(*refs, with_outproj, with_final_norm):
    refs = list(refs)
    x_ref = refs.pop(0)
    if with_outproj:
        mix_ref = refs.pop(0)
        wo_ref = refs.pop(0)
    g_ref, wg_ref, wu_ref, wd_ref = refs[:4]
    refs = refs[4:]
    if with_final_norm:
        fg_ref = refs.pop(0)
    o_ref = refs.pop(0)

    x = x_ref[...]
    if with_outproj:
        x = x + _dot(mix_ref[...], wo_ref[...])
    xn = _rms_norm(x, g_ref[...]).astype(MXU_DTYPE)
    gate = jnp.dot(xn, wg_ref[...], preferred_element_type=jnp.float32)
    up = jnp.dot(xn, wu_ref[...], preferred_element_type=jnp.float32)
    h = (_silu(gate) * up).astype(MXU_DTYPE)
    y = x + FFN_RES_WEIGHT * jnp.dot(h, wd_ref[...], preferred_element_type=jnp.float32)
    if with_final_norm:
        y = _rms_norm(y, fg_ref[...])
    o_ref[...] = y


def _row_tile(m):
    return min(m, 512)


def _ffn(x, norm_g, wg, wu, wd, mix=None, wo=None, final_g=None):
    m = x.shape[0]
    tm = _row_tile(m)
    with_outproj = mix is not None
    with_final_norm = final_g is not None
    row = lambda i: (i, 0)
    args, specs = [x], [pl.BlockSpec((tm, D_MODEL), row)]
    if with_outproj:
        args += [mix, wo]
        specs += [pl.BlockSpec((tm, mix.shape[1]), row), _const_spec(wo.shape)]
    args += [norm_g, wg, wu, wd]
    specs += [_const_spec(norm_g.shape), _const_spec(wg.shape), _const_spec(wu.shape), _const_spec(wd.shape)]
    if with_final_norm:
        args.append(final_g)
        specs.append(_const_spec(final_g.shape))
    return pl.pallas_call(
        functools.partial(_ffn_kernel, with_outproj=with_outproj, with_final_norm=with_final_norm),
        grid=(m // tm,),
        in_specs=specs,
        out_specs=pl.BlockSpec((tm, D_MODEL), row),
        out_shape=jax.ShapeDtypeStruct((m, D_MODEL), jnp.float32),
        compiler_params=pltpu.CompilerParams(dimension_semantics=("arbitrary",), vmem_limit_bytes=VMEM_LIMIT),
        name="ffn_out" if with_outproj else "ffn_in",
    )(*args)


def _inproj_kernel(x_ref, g_ref, w_ref, u_ref, z_ref, xbc_ref, dt_ref):
    xn = _rms_norm(x_ref[...], g_ref[...]).astype(MXU_DTYPE)

    def proj(lo, width):
        return jnp.dot(xn, w_ref[:, lo:lo + width], preferred_element_type=jnp.float32)

    glu_a = proj(0, D_CONV)
    glu_b = proj(D_CONV, D_CONV)
    u_ref[...] = (glu_a * jax.nn.sigmoid(glu_b)).astype(u_ref.dtype)
    z_ref[...] = proj(2 * D_CONV, D_SSM).astype(z_ref.dtype)
    xbc_ref[...] = proj(2 * D_CONV + D_SSM, D_XBC).astype(xbc_ref.dtype)
    dt_ref[...] = proj(2 * D_CONV + D_SSM + D_XBC, DT_PAD)


def _inproj(x, norm_g, w_in_p):
    m = x.shape[0]
    tm = _row_tile(m)
    row = lambda i: (i, 0)
    widths = (D_CONV, D_SSM, D_XBC, DT_PAD)
    dtypes = (ACT_DTYPE, ACT_DTYPE, ACT_DTYPE, jnp.float32)
    return pl.pallas_call(
        _inproj_kernel,
        grid=(m // tm,),
        in_specs=[pl.BlockSpec((tm, D_MODEL), row), _const_spec(norm_g.shape), _const_spec(w_in_p.shape)],
        out_specs=[pl.BlockSpec((tm, w), row) for w in widths],
        out_shape=[jax.ShapeDtypeStruct((m, w), d) for w, d in zip(widths, dtypes)],
        compiler_params=pltpu.CompilerParams(dimension_semantics=("arbitrary",), vmem_limit_bytes=VMEM_LIMIT),
        name="inproj",
    )(x, norm_g, w_in_p)


def _mixer_kernel(u_ref, z_ref, xbc_ref, dt_ref,
                  cwa_ref, cba_ref, lng_ref, lnb_ref,
                  cwb_ref, cbb_ref, dtb_ref, a_ref, dexp_ref, nrm_ref, e_ref,
                  mix_ref, ssm_ref,
                  exta_ref, extb_ref, ht_ref, ua_ref, xc_ref, y_ref):
    c = pl.program_id(1)
    f32 = jnp.float32

    @pl.when(c == 0)
    def _():
        exta_ref[0:A_TAIL, :] = jnp.zeros((A_TAIL, D_CONV), f32)
        extb_ref[0:B_TAIL, :] = jnp.zeros((B_TAIL, D_XBC), f32)
        ht_ref[...] = jnp.zeros_like(ht_ref)

    exta_ref[A_TAIL:A_TAIL + CHUNK, :] = u_ref[0].astype(f32)
    for j in range(D_CONV // LANES):
        lanes = slice(j * LANES, (j + 1) * LANES)
        ua_ref[:, lanes] = _causal_conv_tile(exta_ref, cwa_ref, lanes, A_TAIL, CONV_A_WIDTH) + cba_ref[:, lanes]
    exta_ref[0:A_TAIL, :] = exta_ref[CHUNK:CHUNK + A_TAIL, :]
    ya = _layer_norm_silu(ua_ref[...], lng_ref[...], lnb_ref[...])
    mix_ref[0, :, 0:D_CONV] = ya.astype(mix_ref.dtype)

    extb_ref[B_TAIL:B_TAIL + CHUNK, :] = xbc_ref[0].astype(f32)
    for j in range(D_XBC // LANES):
        lanes = slice(j * LANES, (j + 1) * LANES)
        xc_ref[:, lanes] = _silu(_causal_conv_tile(extb_ref, cwb_ref, lanes, B_TAIL, SSM_CONV_WIDTH) + cbb_ref[:, lanes])
    extb_ref[0:B_TAIL, :] = extb_ref[CHUNK:CHUNK + B_TAIL, :]

    dt = _softplus(dt_ref[0] + dtb_ref[...])
    a = dt * a_ref[...]
    rows = lax.broadcasted_iota(jnp.int32, (CHUNK, CHUNK), 0)
    cols = lax.broadcasted_iota(jnp.int32, (CHUNK, CHUNK), 1)
    causal = rows >= cols
    tri = causal.astype(MXU_DTYPE)
    a_cs = _dot_split_rhs(tri, a)
    a_cs_t = a_cs.T
    dt_t = dt.T
    ea = jnp.exp(a_cs)
    w_end = jnp.exp(a_cs[CHUNK - 1:CHUNK, :] - a_cs) * dt
    e_mat = e_ref[...]
    w_exp = _dot_split_lhs(w_end, e_mat)
    ea_exp = _dot_split_lhs(ea, e_mat)
    xs = xc_ref[:, 0:D_SSM]
    xs_b = xs.astype(MXU_DTYPE)
    xw_b = (xs * w_exp).astype(MXU_DTYPE)
    lane = lax.broadcasted_iota(jnp.int32, (CHUNK, LANES), 1)
    gw = HEADS_PER_GROUP * SSM_HEAD_DIM
    for g in range(SSM_GROUPS):
        b_g = xc_ref[:, D_SSM + g * D_STATE:D_SSM + (g + 1) * D_STATE]
        c_g = xc_ref[:, D_SSM + D_BC + g * D_STATE:D_SSM + D_BC + (g + 1) * D_STATE]
        c_gb = c_g.astype(MXU_DTYPE)
        cb = _dot_nt(c_gb, b_g)
        ht_g = ht_ref[:, g * gw:(g + 1) * gw]
        y_off = _dot(c_gb, ht_g) * ea_exp[:, g * gw:(g + 1) * gw]
        for hp in range(HEADS_PER_GROUP // 2):
            res = []
            for e in range(2):
                h = g * HEADS_PER_GROUP + 2 * hp + e
                seg = a_cs[:, h:h + 1] - a_cs_t[h:h + 1, :]
                l_mat = jnp.exp(jnp.where(causal, seg, NEG_BIG))
                gmat = (cb * l_mat * dt_t[h:h + 1, :]).astype(MXU_DTYPE)
                lo = g * gw + hp * LANES
                res.append(jnp.dot(gmat, xs_b[:, lo:lo + LANES], preferred_element_type=f32))
            y_pair = jnp.where(lane < SSM_HEAD_DIM, res[0], res[1])
            y_ref[:, lo:lo + LANES] = y_pair + y_off[:, hp * LANES:(hp + 1) * LANES]
        s_loc = _dot(b_g.T, xw_b[:, g * gw:(g + 1) * gw])
        ht_ref[:, g * gw:(g + 1) * gw] = ht_g * ea_exp[CHUNK - 1:CHUNK, g * gw:(g + 1) * gw] + s_loc

    y = y_ref[...] + dexp_ref[...] * xs
    ys = _gate_group_norm(y, z_ref[0].astype(f32), nrm_ref[...])
    mix_ref[0, :, D_CONV:D_CONV + D_SSM] = ys.astype(mix_ref.dtype)

    @pl.when(c == pl.num_programs(1) - 1)
    def _():
        for j in range(D_SSM // LANES):
            ssm_ref[0, j * LANES:(j + 1) * LANES, :] = ht_ref[:, j * LANES:(j + 1) * LANES].T


def _mixer(u, z, xbc, dt_raw, p):
    b, l, _ = u.shape
    nc = l // CHUNK
    tok = lambda i, c: (i, c, 0)
    consts = [p["cwa"], p["cba"], p["lng"], p["lnb"], p["cwb"], p["cbb"], p["dtb"], p["a"], p["dexp"], p["nrm"], p["e"]]
    f32 = jnp.float32
    return pl.pallas_call(
        _mixer_kernel,
        grid=(b, nc),
        in_specs=[pl.BlockSpec((1, CHUNK, D_CONV), tok), pl.BlockSpec((1, CHUNK, D_SSM), tok),
                  pl.BlockSpec((1, CHUNK, D_XBC), tok), pl.BlockSpec((1, CHUNK, DT_PAD), tok)]
                 + [_const_spec(a.shape) for a in consts],
        out_specs=[pl.BlockSpec((1, CHUNK, D_CONV + D_SSM), tok),
                   pl.BlockSpec((1, D_SSM, D_STATE), lambda i, c: (i, 0, 0))],
        out_shape=[jax.ShapeDtypeStruct((b, l, D_CONV + D_SSM), ACT_DTYPE),
                   jax.ShapeDtypeStruct((b, D_SSM, D_STATE), f32)],
        scratch_shapes=[pltpu.VMEM((A_TAIL + CHUNK, D_CONV), f32),
                        pltpu.VMEM((B_TAIL + CHUNK, D_XBC), f32),
                        pltpu.VMEM((D_STATE, D_SSM), f32),
                        pltpu.VMEM((CHUNK, D_CONV), f32),
                        pltpu.VMEM((CHUNK, D_XBC), f32),
                        pltpu.VMEM((CHUNK, D_SSM), f32)],
        compiler_params=pltpu.CompilerParams(dimension_semantics=("arbitrary", "arbitrary"),
                                             vmem_limit_bytes=VMEM_LIMIT),
        name="mixer",
    )(u, z, xbc, dt_raw, *consts)


def _sample_conv_a_kernel(st_ref, u_ref, w_ref, cba_ref, lng_ref, lnb_ref, new_ref, ya_ref, acc_ref):
    k = pl.program_id(0)
    hist = CONV_A_WIDTH - 1

    @pl.when(k == 0)
    def _():
        acc_ref[...] = jnp.zeros_like(acc_ref)

    @pl.when(k < hist)
    def _():
        acc_ref[...] += st_ref[...] * w_ref[...]

    @pl.when(jnp.logical_and(k >= 1, k < hist))
    def _():
        new_ref[...] = st_ref[...]

    @pl.when(k == hist)
    def _():
        u = u_ref[...].astype(jnp.float32)
        new_ref[...] = u
        ua = acc_ref[...] + u * w_ref[...] + cba_ref[...]
        ya_ref[...] = _layer_norm_silu(ua, lng_ref[...], lnb_ref[...])


def _sample_conv_a(state, u, p):
    s = state.shape[0]
    hist = CONV_A_WIDTH - 1
    st2 = state.reshape(s, hist * D_CONV)
    w3 = p["cwa"][:CONV_A_WIDTH].reshape(CONV_A_WIDTH, 1, D_CONV)
    new, ya = pl.pallas_call(
        _sample_conv_a_kernel,
        grid=(CONV_A_WIDTH,),
        in_specs=[pl.BlockSpec((s, D_CONV), lambda k: (0, jnp.minimum(k, hist - 1))),
                  _const_spec(u.shape),
                  pl.BlockSpec((None, 1, D_CONV), lambda k: (k, 0, 0)),
                  _const_spec(p["cba"].shape), _const_spec(p["lng"].shape), _const_spec(p["lnb"].shape)],
        out_specs=[pl.BlockSpec((s, D_CONV), lambda k: (0, jnp.maximum(k - 1, 0))),
                   pl.BlockSpec((s, D_CONV), lambda k: (0, 0))],
        out_shape=[jax.ShapeDtypeStruct((s, hist * D_CONV), jnp.float32),
                   jax.ShapeDtypeStruct((s, D_CONV), jnp.float32)],
        scratch_shapes=[pltpu.VMEM((s, D_CONV), jnp.float32)],
        compiler_params=pltpu.CompilerParams(dimension_semantics=("arbitrary",), vmem_limit_bytes=VMEM_LIMIT),
        name="sample_conv_a",
    )(st2, u, w3, p["cba"], p["lng"], p["lnb"])
    return new.reshape(s, hist, D_CONV), ya


def _sample_conv_b_kernel(st_ref, xbc_ref, dt_ref, cwb_ref, cbb_ref, dtb_ref, a_ref, e_ref,
                          new_ref, xs_ref, x_ref, da_ref, bc_ref):
    hist = SSM_CONV_WIDTH - 1
    xbc = xbc_ref[...].astype(jnp.float32)
    acc = xbc * cwb_ref[hist:hist + 1, :] + cbb_ref[...]
    for k in range(hist):
        acc = acc + st_ref[:, k * D_XBC:(k + 1) * D_XBC] * cwb_ref[k:k + 1, :]
    for k in range(hist - 1):
        new_ref[:, k * D_XBC:(k + 1) * D_XBC] = st_ref[:, (k + 1) * D_XBC:(k + 2) * D_XBC]
    new_ref[:, (hist - 1) * D_XBC:hist * D_XBC] = xbc
    xc = _silu(acc)
    xs = xc[:, 0:D_SSM]
    dt = _softplus(dt_ref[...] + dtb_ref[...])
    da = jnp.exp(dt * a_ref[...])
    e_mat = e_ref[...]
    xs_ref[...] = xs
    x_ref[...] = xs * _dot_split_lhs(dt, e_mat)
    da_ref[...] = _dot_split_lhs(da, e_mat)
    bc_ref[...] = xc[:, D_SSM:D_XBC]


def _sample_conv_b(state, xbc, dt_raw, p):
    s = state.shape[0]
    hist = SSM_CONV_WIDTH - 1
    st2 = state.reshape(s, hist * D_XBC)
    f32 = jnp.float32
    args = [st2, xbc, dt_raw, p["cwb"], p["cbb"], p["dtb"], p["a"], p["e"]]
    outs = pl.pallas_call(
        _sample_conv_b_kernel,
        grid=(1,),
        in_specs=[_const_spec(a.shape) for a in args],
        out_specs=[_full_spec(sh) for sh in ((s, hist * D_XBC), (s, D_SSM), (s, D_SSM), (s, D_SSM), (s, 2 * D_BC))],
        out_shape=[jax.ShapeDtypeStruct((s, hist * D_XBC), f32), jax.ShapeDtypeStruct((s, D_SSM), f32),
                   jax.ShapeDtypeStruct((s, D_SSM), f32), jax.ShapeDtypeStruct((s, D_SSM), f32),
                   jax.ShapeDtypeStruct((s, 2 * D_BC), f32)],
        compiler_params=pltpu.CompilerParams(dimension_semantics=("arbitrary",), vmem_limit_bytes=VMEM_LIMIT),
        name="sample_conv_b",
    )(*args)
    new, xs, x_dt, da_exp, bc = outs
    return new.reshape(s, hist, D_XBC), xs, x_dt, da_exp, bc


def _sample_ssd_kernel(st_ref, x_ref, da_ref, b_ref, c_ref, new_ref, y_ref):
    j = pl.program_id(1)
    b_v = b_ref[...]
    c_v = c_ref[...]
    lane = lax.broadcasted_iota(jnp.int32, y_ref.shape, 1)

    def one_head(e):
        base = e * SSM_HEAD_DIM
        da_col = da_ref[:, base:base + 1]
        y_acc = jnp.zeros(y_ref.shape, jnp.float32)
        for pp in range(SSM_HEAD_DIM):
            cols = slice(pp * D_STATE, (pp + 1) * D_STATE)
            h_new = st_ref[:, cols] * da_col + x_ref[:, base + pp:base + pp + 1] * b_v
            new_ref[:, cols] = h_new
            y_col = jnp.sum(h_new * c_v, axis=1, keepdims=True)
            y_acc = jnp.where(lane == base + pp, y_col, y_acc)
        return y_acc

    @pl.when(j == 0)
    def _():
        y_ref[...] = one_head(0)

    @pl.when(j == 1)
    def _():
        y_ref[...] = jnp.where(lane >= SSM_HEAD_DIM, one_head(1), y_ref[...])


def _sample_ssd(state, x_dt, da_exp, bc):
    s = state.shape[0]
    hw = SSM_HEAD_DIM * D_STATE
    st2 = state.reshape(s, SSM_HEADS * hw)
    pairs_per_group = HEADS_PER_GROUP // 2
    pair = lambda i, j: (0, i)
    new, y = pl.pallas_call(
        _sample_ssd_kernel,
        grid=(SSM_HEADS // 2, 2),
        in_specs=[pl.BlockSpec((s, hw), lambda i, j: (0, 2 * i + j)),
                  pl.BlockSpec((s, LANES), pair), pl.BlockSpec((s, LANES), pair),
                  pl.BlockSpec((s, D_STATE), lambda i, j: (0, i // pairs_per_group)),
                  pl.BlockSpec((s, D_STATE), lambda i, j: (0, SSM_GROUPS + i // pairs_per_group))],
        out_specs=[pl.BlockSpec((s, hw), lambda i, j: (0, 2 * i + j)), pl.BlockSpec((s, LANES), pair)],
        out_shape=[jax.ShapeDtypeStruct((s, SSM_HEADS * hw), jnp.float32),
                   jax.ShapeDtypeStruct((s, D_SSM), jnp.float32)],
        compiler_params=pltpu.CompilerParams(dimension_semantics=("arbitrary", "arbitrary"),
                                             vmem_limit_bytes=VMEM_LIMIT),
        name="sample_ssd",
    )(st2, x_dt, da_exp, bc, bc)
    return new.reshape(s, SSM_HEADS, SSM_HEAD_DIM, D_STATE), y


def _sample_gate_kernel(ya_ref, y_ref, xs_ref, z_ref, dexp_ref, nrm_ref, mix_ref):
    y = y_ref[...] + dexp_ref[...] * xs_ref[...]
    ys = _gate_group_norm(y, z_ref[...].astype(jnp.float32), nrm_ref[...])
    mix_ref[:, 0:D_CONV] = ya_ref[...].astype(mix_ref.dtype)
    mix_ref[:, D_CONV:D_CONV + D_SSM] = ys.astype(mix_ref.dtype)


def _sample_gate(ya, y, xs, z, p):
    s = ya.shape[0]
    args = [ya, y, xs, z, p["dexp"], p["nrm"]]
    return pl.pallas_call(
        _sample_gate_kernel,
        grid=(1,),
        in_specs=[_const_spec(a.shape) for a in args],
        out_specs=_full_spec((s, D_CONV + D_SSM)),
        out_shape=jax.ShapeDtypeStruct((s, D_CONV + D_SSM), ACT_DTYPE),
        compiler_params=pltpu.CompilerParams(dimension_semantics=("arbitrary",), vmem_limit_bytes=VMEM_LIMIT),
        name="sample_gate",
    )(*args)


def _prep_params(i, conv_dw_w, conv_dw_b, conv_ln_g, conv_ln_b, ssm_conv_w, ssm_conv_b, ssm_dt_bias, ssm_a_log,
                 ssm_d, ssm_norm):
    f32 = jnp.float32
    row = lambda v: v.reshape(1, -1).astype(f32)
    pad_heads = lambda v: jnp.pad(v.astype(f32), (0, DT_PAD - SSM_HEADS)).reshape(1, DT_PAD)
    head_of_channel = jnp.arange(D_SSM, dtype=jnp.int32) // SSM_HEAD_DIM
    expand = (jnp.arange(DT_PAD, dtype=jnp.int32)[:, None] == head_of_channel[None, :]).astype(MXU_DTYPE)
    return {
        "cwa": jnp.pad(conv_dw_w[i].astype(f32), ((0, A_TAIL - CONV_A_WIDTH), (0, 0))),
        "cba": row(conv_dw_b[i]), "lng": row(conv_ln_g[i]), "lnb": row(conv_ln_b[i]),
        "cwb": jnp.pad(ssm_conv_w[i].astype(f32), ((0, B_TAIL - SSM_CONV_WIDTH), (0, 0))),
        "cbb": row(ssm_conv_b[i]),
        "dtb": pad_heads(ssm_dt_bias[i]),
        "a": pad_heads(-jnp.exp(ssm_a_log[i].astype(f32))),
        "dexp": jnp.repeat(ssm_d[i].astype(f32), SSM_HEAD_DIM).reshape(1, D_SSM),
        "nrm": row(ssm_norm[i]),
        "e": expand,
    }


def kernel(x_prompt, x_sample, state_conv_a, state_conv_b, state_ssm, ffn1_norm, ffn1_w_gate, ffn1_w_up, ffn1_w_down, mix_norm, w_in, conv_dw_w, conv_dw_b, conv_ln_g, conv_ln_b, ssm_conv_w, ssm_conv_b, ssm_dt_bias, ssm_a_log, ssm_d, ssm_norm, w_out, ffn2_norm, ffn2_w_gate, ffn2_w_up, ffn2_w_down, final_norm):
    depth = ffn1_norm.shape[0]
    bp, lp, _ = x_prompt.shape
    bs = x_sample.shape[0]
    row = lambda v: v.reshape(1, -1).astype(jnp.float32)
    w16 = lambda w: w.astype(MXU_DTYPE)

    yp = x_prompt.reshape(bp * lp, D_MODEL)
    ysm = x_sample.reshape(bs, D_MODEL)
    outs = [[] for _ in range(6)]
    for i in range(depth):
        p = _prep_params(i, conv_dw_w, conv_dw_b, conv_ln_g, conv_ln_b, ssm_conv_w, ssm_conv_b, ssm_dt_bias,
                         ssm_a_log, ssm_d, ssm_norm)
        w_in_p = w16(jnp.pad(w_in[i], ((0, 0), (0, DT_PAD - SSM_HEADS))))
        f1 = (row(ffn1_norm[i]), w16(ffn1_w_gate[i]), w16(ffn1_w_up[i]), w16(ffn1_w_down[i]))
        f2 = (row(ffn2_norm[i]), w16(ffn2_w_gate[i]), w16(ffn2_w_up[i]), w16(ffn2_w_down[i]))
        wo = w16(w_out[i])
        fin = row(final_norm) if i == depth - 1 else None

        x1 = _ffn(yp, *f1)
        u, z, xbc, dt_raw = _inproj(x1, row(mix_norm[i]), w_in_p)
        mix, ssm_t = _mixer(u.reshape(bp, lp, D_CONV), z.reshape(bp, lp, D_SSM), xbc.reshape(bp, lp, D_XBC),
                            dt_raw.reshape(bp, lp, DT_PAD), p)
        yp = _ffn(x1, *f2, mix=mix.reshape(bp * lp, D_CONV + D_SSM), wo=wo, final_g=fin)
        outs[0].append(u.reshape(bp, lp, D_CONV)[:, lp - (CONV_A_WIDTH - 1):, :].astype(x_prompt.dtype))
        outs[1].append(xbc.reshape(bp, lp, D_XBC)[:, lp - (SSM_CONV_WIDTH - 1):, :].astype(x_prompt.dtype))
        outs[2].append(ssm_t.reshape(bp, SSM_HEADS, SSM_HEAD_DIM, D_STATE).astype(state_ssm.dtype))

        s1 = _ffn(ysm, *f1)
        su, sz, sxbc, sdt = _inproj(s1, row(mix_norm[i]), w_in_p)
        new_a, ya = _sample_conv_a(state_conv_a[i], su, p)
        new_b, xs, x_dt, da_exp, bc = _sample_conv_b(state_conv_b[i], sxbc, sdt, p)
        new_s, y = _sample_ssd(state_ssm[i].astype(jnp.float32), x_dt, da_exp, bc)
        smix = _sample_gate(ya, y, xs, sz, p)
        ysm = _ffn(s1, *f2, mix=smix, wo=wo, final_g=fin)
        outs[3].append(new_a.astype(x_sample.dtype))
        outs[4].append(new_b.astype(x_sample.dtype))
        outs[5].append(new_s.astype(state_ssm.dtype))

    return (yp.reshape(bp, lp, D_MODEL), ysm.reshape(bs, 1, D_MODEL)) + tuple(jnp.stack(o) for o in outs)
```

```python
import functools

import jax
import jax.numpy as jnp
from jax import lax
from jax.experimental import pallas as pl
from jax.experimental.pallas import tpu as pltpu

D_MODEL = 1024
D_FF = 2816
D_CONV = 1024
CONV_A_WIDTH = 31
D_SSM = 1024
SSM_HEAD_DIM = 64
SSM_HEADS = 16
SSM_GROUPS = 2
HEADS_PER_GROUP = SSM_HEADS // SSM_GROUPS
GROUP_WIDTH = HEADS_PER_GROUP * SSM_HEAD_DIM
D_STATE = 128
SSM_CONV_WIDTH = 4
CHUNK = 128
D_XBC = D_SSM + 2 * SSM_GROUPS * D_STATE
D_BC = SSM_GROUPS * D_STATE
FFN_RES_WEIGHT = 0.5
NORM_EPS = 1e-5

LANES = 128
SUBLANES = 8
DT_PAD = LANES
A_TAIL = 32
B_TAIL = 8
ROW_TILE = 512
CONV_ROWS = 128
SAMPLE_SSD_BLOCK = 8
SAMPLE_CONV_BLOCK = 16
VMEM_LIMIT = 56 * 1024 * 1024
NEG_BIG = -1e30

ACT_DTYPE = jnp.float32
MXU_DTYPE = jnp.bfloat16


def _dot(a, b):
    return jnp.dot(a.astype(MXU_DTYPE), b.astype(MXU_DTYPE), preferred_element_type=jnp.float32)


def _dot_nt(a, b):
    return lax.dot_general(a.astype(MXU_DTYPE), b.astype(MXU_DTYPE), (((1,), (1,)), ((), ())),
                           preferred_element_type=jnp.float32)


def _split2(v):
    hi = v.astype(MXU_DTYPE)
    lo = (v - hi.astype(jnp.float32)).astype(MXU_DTYPE)
    return hi, lo


def _dot_split_lhs(v, m):
    hi, lo = _split2(v)
    return (jnp.dot(hi, m, preferred_element_type=jnp.float32)
            + jnp.dot(lo, m, preferred_element_type=jnp.float32))


def _dot_split_rhs(m, v):
    hi = v.astype(MXU_DTYPE)
    r1 = v - hi.astype(jnp.float32)
    mid = r1.astype(MXU_DTYPE)
    lo = (r1 - mid.astype(jnp.float32)).astype(MXU_DTYPE)
    return (jnp.dot(m, hi, preferred_element_type=jnp.float32)
            + jnp.dot(m, mid, preferred_element_type=jnp.float32)
            + jnp.dot(m, lo, preferred_element_type=jnp.float32))


def _rms_norm(x, g):
    return x * lax.rsqrt(jnp.mean(x * x, axis=-1, keepdims=True) + NORM_EPS) * g


def _silu(x):
    return x * jax.nn.sigmoid(x)


def _softplus(x):
    return jnp.maximum(x, 0.0) + jnp.log1p(jnp.exp(-jnp.abs(x)))


def _layer_norm_silu(x, g, b):
    mu = jnp.mean(x, axis=-1, keepdims=True)
    xc = x - mu
    var = jnp.mean(xc * xc, axis=-1, keepdims=True)
    return _silu(xc * lax.rsqrt(var + NORM_EPS) * g + b)


def _gate_group_norm(y, z, g):
    y = y * _silu(z)
    parts = []
    for i in range(SSM_GROUPS):
        yg = y[:, i * GROUP_WIDTH:(i + 1) * GROUP_WIDTH]
        parts.append(yg * lax.rsqrt(jnp.mean(yg * yg, axis=-1, keepdims=True) + NORM_EPS))
    return jnp.concatenate(parts, axis=-1) * g


def _causal_conv_tile(ext_ref, w_ref, lanes, row0, n_out, tail, width):
    n_rows = tail + n_out
    x = ext_ref[row0:row0 + n_rows, lanes]
    first = tail - (width - 1)
    acc = jnp.zeros((n_out, LANES), jnp.float32)
    for r in range(SUBLANES):
        taps = [k for k in range(width) if (first + k) % SUBLANES == r]
        if not taps:
            continue
        xr = x if r == 0 else pltpu.roll(x, n_rows - r, 0)
        for k in taps:
            base = first + k - r
            acc = acc + xr[base:base + n_out, :] * w_ref[k:k + 1, lanes]
    return acc


def _const_spec(shape):
    nd = len(shape)
    return pl.BlockSpec(shape, lambda *_: (0,) * nd, pipeline_mode=pl.Buffered(1))


def _full_spec(shape):
    nd = len(shape)
    return pl.BlockSpec(shape, lambda *_: (0,) * nd)


def _params(*semantics):
    return pltpu.CompilerParams(dimension_semantics=semantics, vmem_limit_bytes=VMEM_LIMIT)


def _ffn_kernel(*refs, with_outproj, with_final_norm):
    refs = list(refs)
    x_ref = refs.pop(0)
    if with_outproj:
        ya_ref, ys_ref, wo_ref = refs[:3]
        refs = refs[3:]
    g_ref, wg_ref, wu_ref, wd_ref = refs[:4]
    refs = refs[4:]
    if with_final_norm:
        fg_ref = refs.pop(0)
    o_ref = refs.pop(0)

    x = x_ref[...]
    if with_outproj:
        x = x + _dot(ya_ref[...], wo_ref[0:D_CONV, :]) + _dot(ys_ref[...], wo_ref[D_CONV:D_CONV + D_SSM, :])
    xn = _rms_norm(x, g_ref[...]).astype(MXU_DTYPE)
    gate = jnp.dot(xn, wg_ref[...], preferred_element_type=jnp.float32)
    up = jnp.dot(xn, wu_ref[...], preferred_element_type=jnp.float32)
    h = (_silu(gate) * up).astype(MXU_DTYPE)
    y = x + FFN_RES_WEIGHT * jnp.dot(h, wd_ref[...], preferred_element_type=jnp.float32)
    if with_final_norm:
        y = _rms_norm(y, fg_ref[...])
    o_ref[...] = y


def _ffn(x, norm_g, wg, wu, wd, ya=None, ys=None, wo=None, final_g=None):
    m = x.shape[0]
    tm = min(m, ROW_TILE)
    with_outproj = ya is not None
    with_final_norm = final_g is not None
    row = lambda i: (i, 0)
    args, specs = [x], [pl.BlockSpec((tm, D_MODEL), row)]
    if with_outproj:
        args += [ya, ys, wo]
        specs += [pl.BlockSpec((tm, D_CONV), row), pl.BlockSpec((tm, D_SSM), row), _const_spec(wo.shape)]
    args += [norm_g, wg, wu, wd]
    specs += [_const_spec(norm_g.shape), _const_spec(wg.shape), _const_spec(wu.shape), _const_spec(wd.shape)]
    if with_final_norm:
        args.append(final_g)
        specs.append(_const_spec(final_g.shape))
    return pl.pallas_call(
        functools.partial(_ffn_kernel, with_outproj=with_outproj, with_final_norm=with_final_norm),
        grid=(m // tm,),
        in_specs=specs,
        out_specs=pl.BlockSpec((tm, D_MODEL), row),
        out_shape=jax.ShapeDtypeStruct((m, D_MODEL), jnp.float32),
        compiler_params=_params("arbitrary"),
        name="ffn_out" if with_outproj else "ffn_in",
    )(*args)


def _project(x_ref, g_ref, w_ref, z_ref, xbc_ref, dt_ref):
    xn = _rms_norm(x_ref[...], g_ref[...]).astype(MXU_DTYPE)

    def proj(lo, width):
        return jnp.dot(xn, w_ref[:, lo:lo + width], preferred_element_type=jnp.float32)

    u = proj(0, D_CONV) * jax.nn.sigmoid(proj(D_CONV, D_CONV))
    z_ref[...] = proj(2 * D_CONV, D_SSM).astype(z_ref.dtype)
    xbc_ref[...] = proj(2 * D_CONV + D_SSM, D_XBC).astype(xbc_ref.dtype)
    dt_ref[...] = proj(2 * D_CONV + D_SSM + D_XBC, DT_PAD)
    return u


def _inproj_kernel(x_ref, g_ref, w_ref, u_ref, z_ref, xbc_ref, dt_ref):
    u_ref[...] = _project(x_ref, g_ref, w_ref, z_ref, xbc_ref, dt_ref)


def _inproj_conv_kernel(x_ref, g_ref, w_ref, cwa_ref, cba_ref, lng_ref, lnb_ref,
                        ya_ref, tail_ref, z_ref, xbc_ref, dt_ref, ext_ref, ua_ref, *, tiles_per_seq):
    tm = x_ref.shape[0]

    @pl.when(pl.program_id(0) % tiles_per_seq == 0)
    def _():
        ext_ref[0:A_TAIL, :] = jnp.zeros((A_TAIL, D_CONV), jnp.float32)

    ext_ref[A_TAIL:A_TAIL + tm, :] = _project(x_ref, g_ref, w_ref, z_ref, xbc_ref, dt_ref)
    for c in range(tm // CONV_ROWS):
        for j in range(D_CONV // LANES):
            lanes = slice(j * LANES, (j + 1) * LANES)
            ua_ref[c * CONV_ROWS:(c + 1) * CONV_ROWS, lanes] = (
                _causal_conv_tile(ext_ref, cwa_ref, lanes, c * CONV_ROWS, CONV_ROWS, A_TAIL, CONV_A_WIDTH)
                + cba_ref[:, lanes])
    tail = ext_ref[tm:tm + A_TAIL, :]
    tail_ref[0] = tail
    ext_ref[0:A_TAIL, :] = tail
    ya_ref[...] = _layer_norm_silu(ua_ref[...], lng_ref[...], lnb_ref[...]).astype(ya_ref.dtype)


def _inproj(x, norm_g, w_in_p):
    m = x.shape[0]
    tm = min(m, ROW_TILE)
    row = lambda i: (i, 0)
    widths = (D_CONV, D_SSM, D_XBC, DT_PAD)
    dtypes = (jnp.float32, ACT_DTYPE, ACT_DTYPE, jnp.float32)
    return pl.pallas_call(
        _inproj_kernel,
        grid=(m // tm,),
        in_specs=[pl.BlockSpec((tm, D_MODEL), row), _const_spec(norm_g.shape), _const_spec(w_in_p.shape)],
        out_specs=[pl.BlockSpec((tm, w), row) for w in widths],
        out_shape=[jax.ShapeDtypeStruct((m, w), d) for w, d in zip(widths, dtypes)],
        compiler_params=_params("arbitrary"),
        name="inproj",
    )(x, norm_g, w_in_p)


def _inproj_conv(x, norm_g, w_in_p, conv, seq_len):
    m = x.shape[0]
    tm = min(seq_len, ROW_TILE)
    tiles_per_seq = seq_len // tm
    f32 = jnp.float32
    row = lambda i: (i, 0)
    return pl.pallas_call(
        functools.partial(_inproj_conv_kernel, tiles_per_seq=tiles_per_seq),
        grid=(m // tm,),
        in_specs=[pl.BlockSpec((tm, D_MODEL), row), _const_spec(norm_g.shape), _const_spec(w_in_p.shape)]
                 + [_const_spec(a.shape) for a in conv],
        out_specs=[pl.BlockSpec((tm, D_CONV), row),
                   pl.BlockSpec((1, A_TAIL, D_CONV), lambda i: (i // tiles_per_seq, 0, 0))]
                  + [pl.BlockSpec((tm, w), row) for w in (D_SSM, D_XBC, DT_PAD)],
        out_shape=[jax.ShapeDtypeStruct((m, D_CONV), ACT_DTYPE),
                   jax.ShapeDtypeStruct((m // seq_len, A_TAIL, D_CONV), f32),
                   jax.ShapeDtypeStruct((m, D_SSM), ACT_DTYPE), jax.ShapeDtypeStruct((m, D_XBC), ACT_DTYPE),
                   jax.ShapeDtypeStruct((m, DT_PAD), f32)],
        scratch_shapes=[pltpu.VMEM((A_TAIL + tm, D_CONV), f32), pltpu.VMEM((tm, D_CONV), f32)],
        compiler_params=_params("arbitrary"),
        name="inproj_conv",
    )(x, norm_g, w_in_p, *conv)


def _ssd_kernel(z_ref, xbc_ref, dt_ref,
                cwb_ref, cbb_ref, dtb_ref, a_ref, dexp_ref, nrm_ref, e_ref,
                ys_ref, ssm_ref,
                extb_ref, ht_ref, xc_ref, y_ref):
    c = pl.program_id(1)
    f32 = jnp.float32

    @pl.when(c == 0)
    def _():
        extb_ref[0:B_TAIL, :] = jnp.zeros((B_TAIL, D_XBC), f32)
        ht_ref[...] = jnp.zeros_like(ht_ref)

    extb_ref[B_TAIL:B_TAIL + CHUNK, :] = xbc_ref[0].astype(f32)
    for j in range(D_XBC // LANES):
        lanes = slice(j * LANES, (j + 1) * LANES)
        xc_ref[:, lanes] = _silu(_causal_conv_tile(extb_ref, cwb_ref, lanes, 0, CHUNK, B_TAIL, SSM_CONV_WIDTH)
                                 + cbb_ref[:, lanes])
    extb_ref[0:B_TAIL, :] = extb_ref[CHUNK:CHUNK + B_TAIL, :]

    dt = _softplus(dt_ref[0] + dtb_ref[...])
    a = dt * a_ref[...]
    rows = lax.broadcasted_iota(jnp.int32, (CHUNK, CHUNK), 0)
    cols = lax.broadcasted_iota(jnp.int32, (CHUNK, CHUNK), 1)
    causal = rows >= cols
    tri = causal.astype(MXU_DTYPE)
    a_cs = _dot_split_rhs(tri, a)
    a_cs_t = a_cs.T
    dt_t = dt.T
    ea = jnp.exp(a_cs)
    w_end = jnp.exp(a_cs[CHUNK - 1:CHUNK, :] - a_cs) * dt
    e_mat = e_ref[...]
    w_exp = _dot_split_lhs(w_end, e_mat)
    ea_exp = _dot_split_lhs(ea, e_mat)
    xs = xc_ref[:, 0:D_SSM]
    xs_b = xs.astype(MXU_DTYPE)
    xw_b = (xs * w_exp).astype(MXU_DTYPE)
    lane = lax.broadcasted_iota(jnp.int32, (CHUNK, LANES), 1)
    gw = GROUP_WIDTH
    for g in range(SSM_GROUPS):
        b_g = xc_ref[:, D_SSM + g * D_STATE:D_SSM + (g + 1) * D_STATE]
        c_g = xc_ref[:, D_SSM + D_BC + g * D_STATE:D_SSM + D_BC + (g + 1) * D_STATE]
        c_gb = c_g.astype(MXU_DTYPE)
        cb = _dot_nt(c_gb, b_g)
        ht_g = ht_ref[:, g * gw:(g + 1) * gw]
        y_off = _dot(c_gb, ht_g) * ea_exp[:, g * gw:(g + 1) * gw]
        for hp in range(HEADS_PER_GROUP // 2):
            res = []
            for e in range(2):
                h = g * HEADS_PER_GROUP + 2 * hp + e
                seg = a_cs[:, h:h + 1] - a_cs_t[h:h + 1, :]
                l_mat = jnp.exp(jnp.where(causal, seg, NEG_BIG))
                gmat = (cb * l_mat * dt_t[h:h + 1, :]).astype(MXU_DTYPE)
                lo = g * gw + hp * LANES
                res.append(jnp.dot(gmat, xs_b[:, lo:lo + LANES], preferred_element_type=f32))
            y_pair = jnp.where(lane < SSM_HEAD_DIM, res[0], res[1])
            y_ref[:, lo:lo + LANES] = y_pair + y_off[:, hp * LANES:(hp + 1) * LANES]
        s_loc = _dot(b_g.T, xw_b[:, g * gw:(g + 1) * gw])
        ht_ref[:, g * gw:(g + 1) * gw] = ht_g * ea_exp[CHUNK - 1:CHUNK, g * gw:(g + 1) * gw] + s_loc

    y = y_ref[...] + dexp_ref[...] * xs
    ys_ref[0] = _gate_group_norm(y, z_ref[0].astype(f32), nrm_ref[...]).astype(ys_ref.dtype)

    @pl.when(c == pl.num_programs(1) - 1)
    def _():
        for j in range(D_SSM // LANES):
            ssm_ref[0, j * LANES:(j + 1) * LANES, :] = ht_ref[:, j * LANES:(j + 1) * LANES].T


def _ssd(z, xbc, dt_raw, p):
    b, l, _ = z.shape
    nc = l // CHUNK
    tok = lambda i, c: (i, c, 0)
    consts = [p["cwb"], p["cbb"], p["dtb"], p["a"], p["dexp"], p["nrm"], p["e"]]
    f32 = jnp.float32
    return pl.pallas_call(
        _ssd_kernel,
        grid=(b, nc),
        in_specs=[pl.BlockSpec((1, CHUNK, D_SSM), tok), pl.BlockSpec((1, CHUNK, D_XBC), tok),
                  pl.BlockSpec((1, CHUNK, DT_PAD), tok)] + [_const_spec(a.shape) for a in consts],
        out_specs=[pl.BlockSpec((1, CHUNK, D_SSM), tok),
                   pl.BlockSpec((1, D_SSM, D_STATE), lambda i, c: (i, 0, 0))],
        out_shape=[jax.ShapeDtypeStruct((b, l, D_SSM), ACT_DTYPE),
                   jax.ShapeDtypeStruct((b, D_SSM, D_STATE), f32)],
        scratch_shapes=[pltpu.VMEM((B_TAIL + CHUNK, D_XBC), f32),
                        pltpu.VMEM((D_STATE, D_SSM), f32),
                        pltpu.VMEM((CHUNK, D_XBC), f32),
                        pltpu.VMEM((CHUNK, D_SSM), f32)],
        compiler_params=_params("arbitrary", "arbitrary"),
        name="ssd",
    )(z, xbc, dt_raw, *consts)


def _sample_conv_a_kernel(st_ref, u_ref, w_ref, cba_ref, lng_ref, lnb_ref, new_ref, ya_ref, ua_ref):
    hist = CONV_A_WIDTH - 1
    w_hist = w_ref[0:hist, :]
    w_last = w_ref[hist:hist + 1, :]

    def body(s, carry):
        u_row = u_ref[pl.ds(s, 1), :]
        ua_ref[pl.ds(s, 1), :] = jnp.sum(st_ref[s] * w_hist, axis=0, keepdims=True) + u_row * w_last
        new_ref[s, 0:hist - 1, :] = st_ref[s, 1:hist, :]
        new_ref[s, hist - 1:hist, :] = u_row
        return carry

    lax.fori_loop(0, st_ref.shape[0], body, 0)
    ya_ref[...] = _layer_norm_silu(ua_ref[...] + cba_ref[...], lng_ref[...], lnb_ref[...])


def _sample_conv_a(state, u, p):
    s = state.shape[0]
    hist = CONV_A_WIDTH - 1
    bs = SAMPLE_CONV_BLOCK
    f32 = jnp.float32
    blk3 = pl.BlockSpec((bs, hist, D_CONV), lambda i: (i, 0, 0))
    blk2 = pl.BlockSpec((bs, D_CONV), lambda i: (i, 0))
    return pl.pallas_call(
        _sample_conv_a_kernel,
        grid=(s // bs,),
        in_specs=[blk3, blk2, _const_spec(p["cwa"].shape), _const_spec(p["cba"].shape),
                  _const_spec(p["lng"].shape), _const_spec(p["lnb"].shape)],
        out_specs=[blk3, blk2],
        out_shape=[jax.ShapeDtypeStruct((s, hist, D_CONV), f32), jax.ShapeDtypeStruct((s, D_CONV), f32)],
        scratch_shapes=[pltpu.VMEM((bs, D_CONV), f32)],
        compiler_params=_params("arbitrary"),
        name="sample_conv_a",
    )(state, u, p["cwa"], p["cba"], p["lng"], p["lnb"])


def _sample_conv_b_kernel(st_ref, xbc_ref, dt_ref, cwb_ref, cbb_ref, dtb_ref, a_ref, e_ref,
                          new_ref, xs_ref, xt_ref, dat_ref, b_ref, ct_ref):
    hist = SSM_CONV_WIDTH - 1
    xbc = xbc_ref[...].astype(jnp.float32)
    acc = xbc * cwb_ref[hist:hist + 1, :] + cbb_ref[...]
    for k in range(hist):
        acc = acc + st_ref[:, k * D_XBC:(k + 1) * D_XBC] * cwb_ref[k:k + 1, :]
    for k in range(hist - 1):
        new_ref[:, k * D_XBC:(k + 1) * D_XBC] = st_ref[:, (k + 1) * D_XBC:(k + 2) * D_XBC]
    new_ref[:, (hist - 1) * D_XBC:hist * D_XBC] = xbc
    xc = _silu(acc)
    xs = xc[:, 0:D_SSM]
    dt = _softplus(dt_ref[...] + dtb_ref[...])
    da = jnp.exp(dt * a_ref[...])
    xs_ref[...] = xs
    x_dt = xs * _dot_split_lhs(dt, e_ref[...])
    for j in range(D_SSM // LANES):
        xt_ref[j * LANES:(j + 1) * LANES, :] = x_dt[:, j * LANES:(j + 1) * LANES].T.astype(xt_ref.dtype)
    dat_ref[...] = da.T[0:SSM_HEADS, :]
    b_ref[...] = xc[:, D_SSM:D_SSM + D_BC]
    for g in range(SSM_GROUPS):
        lo = D_SSM + D_BC + g * D_STATE
        ct_ref[g * D_STATE:(g + 1) * D_STATE, :] = xc[:, lo:lo + D_STATE].T.astype(ct_ref.dtype)


def _sample_conv_b(state, xbc, dt_raw, p):
    s = state.shape[0]
    hist = SSM_CONV_WIDTH - 1
    st2 = state.reshape(s, hist * D_XBC)
    f32 = jnp.float32
    args = [st2, xbc, dt_raw, p["cwb"], p["cbb"], p["dtb"], p["a"], p["e"]]
    shapes = [((s, hist * D_XBC), f32), ((s, D_SSM), f32), ((D_SSM, s), MXU_DTYPE), ((SSM_HEADS, s), f32),
              ((s, D_BC), f32), ((D_BC, s), MXU_DTYPE)]
    outs = pl.pallas_call(
        _sample_conv_b_kernel,
        grid=(1,),
        in_specs=[_const_spec(a.shape) for a in args],
        out_specs=[_full_spec(sh) for sh, _ in shapes],
        out_shape=[jax.ShapeDtypeStruct(sh, d) for sh, d in shapes],
        compiler_params=_params("arbitrary"),
        name="sample_conv_b",
    )(*args)
    new, xs, xt, dat, b, ct = outs
    return new.reshape(s, hist, D_XBC), xs, xt, dat, b, ct


def _sample_ssd_kernel(dat_ref, st_ref, xt_ref, b_ref, ct_ref, new_ref, yt_ref):
    i = pl.program_id(0)
    bs = st_ref.shape[0]
    n_seq = yt_ref.shape[1]
    f32 = jnp.float32

    @pl.when(i == 0)
    def _():
        yt_ref[...] = jnp.zeros_like(yt_ref)

    row_id = lax.broadcasted_iota(jnp.int32, (n_seq, D_STATE), 0)
    col_id = lax.broadcasted_iota(jnp.int32, (GROUP_WIDTH, n_seq), 1)

    def body(sl, carry):
        s = i * bs + sl
        for g in range(SSM_GROUPS):
            rows = slice(g * GROUP_WIDTH, (g + 1) * GROUP_WIDTH)
            b_sel = jnp.where(row_id == s, b_ref[:, g * D_STATE:(g + 1) * D_STATE], 0.0).astype(MXU_DTYPE)
            outer = jnp.dot(xt_ref[rows, :], b_sel, preferred_element_type=f32)
            h_new = []
            for hh in range(HEADS_PER_GROUP):
                h = g * HEADS_PER_GROUP + hh
                hn = st_ref[sl, h] * dat_ref[h, s] + outer[hh * SSM_HEAD_DIM:(hh + 1) * SSM_HEAD_DIM, :]
                new_ref[sl, h] = hn
                h_new.append(hn.astype(MXU_DTYPE))
            y_all = jnp.dot(jnp.concatenate(h_new, axis=0), ct_ref[g * D_STATE:(g + 1) * D_STATE, :],
                            preferred_element_type=f32)
            yt_ref[rows, :] = jnp.where(col_id == s, y_all, yt_ref[rows, :])
        return carry

    lax.fori_loop(0, bs, body, 0)


def _sample_ssd(state, xt, dat, b, ct):
    s = state.shape[0]
    bs = SAMPLE_SSD_BLOCK
    blk = pl.BlockSpec((bs, SSM_HEADS, SSM_HEAD_DIM, D_STATE), lambda i: (i, 0, 0, 0))
    return pl.pallas_call(
        _sample_ssd_kernel,
        grid=(s // bs,),
        in_specs=[pl.BlockSpec(memory_space=pltpu.SMEM), blk,
                  _const_spec(xt.shape), _const_spec(b.shape), _const_spec(ct.shape)],
        out_specs=[blk, _full_spec((D_SSM, s))],
        out_shape=[jax.ShapeDtypeStruct(state.shape, jnp.float32), jax.ShapeDtypeStruct((D_SSM, s), jnp.float32)],
        compiler_params=_params("arbitrary"),
        name="sample_ssd",
    )(dat, state, xt, b, ct)


def _sample_gate_kernel(yt_ref, xs_ref, z_ref, dexp_ref, nrm_ref, ys_ref):
    y = jnp.concatenate([yt_ref[j * LANES:(j + 1) * LANES, :].T for j in range(D_SSM // LANES)], axis=-1)
    y = y + dexp_ref[...] * xs_ref[...]
    ys_ref[...] = _gate_group_norm(y, z_ref[...].astype(jnp.float32), nrm_ref[...]).astype(ys_ref.dtype)


def _sample_gate(yt, xs, z, p):
    s = xs.shape[0]
    args = [yt, xs, z, p["dexp"], p["nrm"]]
    return pl.pallas_call(
        _sample_gate_kernel,
        grid=(1,),
        in_specs=[_const_spec(a.shape) for a in args],
        out_specs=_full_spec((s, D_SSM)),
        out_shape=jax.ShapeDtypeStruct((s, D_SSM), ACT_DTYPE),
        compiler_params=_params("arbitrary"),
        name="sample_gate",
    )(*args)


def _prep_params(i, conv_dw_w, conv_dw_b, conv_ln_g, conv_ln_b, ssm_conv_w, ssm_conv_b, ssm_dt_bias, ssm_a_log,
                 ssm_d, ssm_norm):
    f32 = jnp.float32
    row = lambda v: v.reshape(1, -1).astype(f32)
    pad_heads = lambda v: jnp.pad(v.astype(f32), (0, DT_PAD - SSM_HEADS)).reshape(1, DT_PAD)
    head_of_channel = jnp.arange(D_SSM, dtype=jnp.int32) // SSM_HEAD_DIM
    expand = (jnp.arange(DT_PAD, dtype=jnp.int32)[:, None] == head_of_channel[None, :]).astype(MXU_DTYPE)
    return {
        "cwa": jnp.pad(conv_dw_w[i].astype(f32), ((0, A_TAIL - CONV_A_WIDTH), (0, 0))),
        "cba": row(conv_dw_b[i]), "lng": row(conv_ln_g[i]), "lnb": row(conv_ln_b[i]),
        "cwb": jnp.pad(ssm_conv_w[i].astype(f32), ((0, B_TAIL - SSM_CONV_WIDTH), (0, 0))),
        "cbb": row(ssm_conv_b[i]),
        "dtb": pad_heads(ssm_dt_bias[i]),
        "a": pad_heads(-jnp.exp(ssm_a_log[i].astype(f32))),
        "dexp": jnp.repeat(ssm_d[i].astype(f32), SSM_HEAD_DIM).reshape(1, D_SSM),
        "nrm": row(ssm_norm[i]),
        "e": expand,
    }


def kernel(x_prompt, x_sample, state_conv_a, state_conv_b, state_ssm, ffn1_norm, ffn1_w_gate, ffn1_w_up, ffn1_w_down, mix_norm, w_in, conv_dw_w, conv_dw_b, conv_ln_g, conv_ln_b, ssm_conv_w, ssm_conv_b, ssm_dt_bias, ssm_a_log, ssm_d, ssm_norm, w_out, ffn2_norm, ffn2_w_gate, ffn2_w_up, ffn2_w_down, final_norm):
    depth = ffn1_norm.shape[0]
    bp, lp, _ = x_prompt.shape
    bs = x_sample.shape[0]
    f32 = jnp.float32
    row = lambda v: v.reshape(1, -1).astype(f32)
    w16 = lambda w: w.astype(MXU_DTYPE)

    yp = x_prompt.reshape(bp * lp, D_MODEL)
    ysm = x_sample.reshape(bs, D_MODEL)
    outs = [[] for _ in range(6)]
    for i in range(depth):
        p = _prep_params(i, conv_dw_w, conv_dw_b, conv_ln_g, conv_ln_b, ssm_conv_w, ssm_conv_b, ssm_dt_bias,
                         ssm_a_log, ssm_d, ssm_norm)
        w_in_p = w16(jnp.pad(w_in[i], ((0, 0), (0, DT_PAD - SSM_HEADS))))
        f1 = (row(ffn1_norm[i]), w16(ffn1_w_gate[i]), w16(ffn1_w_up[i]), w16(ffn1_w_down[i]))
        f2 = (row(ffn2_norm[i]), w16(ffn2_w_gate[i]), w16(ffn2_w_up[i]), w16(ffn2_w_down[i]))
        wo = w16(w_out[i])
        fin = row(final_norm) if i == depth - 1 else None

        x1 = _ffn(yp, *f1)
        ya, a_tail, z, xbc, dt_raw = _inproj_conv(x1, row(mix_norm[i]), w_in_p,
                                                  (p["cwa"], p["cba"], p["lng"], p["lnb"]), lp)
        xbc3 = xbc.reshape(bp, lp, D_XBC)
        ys, ssm_t = _ssd(z.reshape(bp, lp, D_SSM), xbc3, dt_raw.reshape(bp, lp, DT_PAD), p)
        yp = _ffn(x1, *f2, ya=ya, ys=ys.reshape(bp * lp, D_SSM), wo=wo, final_g=fin)
        outs[0].append(a_tail[:, A_TAIL - (CONV_A_WIDTH - 1):, :].astype(x_prompt.dtype))
        outs[1].append(xbc3[:, lp - (SSM_CONV_WIDTH - 1):, :].astype(x_prompt.dtype))
        outs[2].append(ssm_t.reshape(bp, SSM_HEADS, SSM_HEAD_DIM, D_STATE).astype(state_ssm.dtype))

        s1 = _ffn(ysm, *f1)
        su, sz, sxbc, sdt = _inproj(s1, row(mix_norm[i]), w_in_p)
        new_a, sya = _sample_conv_a(state_conv_a[i].astype(f32), su, p)
        new_b, xs, xt, dat, b_mat, ct = _sample_conv_b(state_conv_b[i].astype(f32), sxbc, sdt, p)
        new_s, yt = _sample_ssd(state_ssm[i].astype(f32), xt, dat, b_mat, ct)
        sys_ = _sample_gate(yt, xs, sz, p)
        ysm = _ffn(s1, *f2, ya=sya, ys=sys_, wo=wo, final_g=fin)
        outs[3].append(new_a.astype(x_sample.dtype))
        outs[4].append(new_b.astype(x_sample.dtype))
        outs[5].append(new_s.astype(state_ssm.dtype))

    return (yp.reshape(bp, lp, D_MODEL), ysm.reshape(bs, 1, D_MODEL)) + tuple(jnp.stack(o) for o in outs)
```

```python
import functools

import jax
import jax.numpy as jnp
from jax import lax
from jax.experimental import pallas as pl
from jax.experimental.pallas import tpu as pltpu

D_MODEL = 1024
D_FF = 2816
D_CONV = 1024
CONV_A_WIDTH = 31
D_SSM = 1024
SSM_HEAD_DIM = 64
SSM_HEADS = 16
SSM_GROUPS = 2
HEADS_PER_GROUP = SSM_HEADS // SSM_GROUPS
GROUP_WIDTH = HEADS_PER_GROUP * SSM_HEAD_DIM
D_STATE = 128
SSM_CONV_WIDTH = 4
CHUNK = 128
D_XBC = D_SSM + 2 * SSM_GROUPS * D_STATE
D_BC = SSM_GROUPS * D_STATE
FFN_RES_WEIGHT = 0.5
NORM_EPS = 1e-5

LANES = 128
SUBLANES = 8
DT_PAD = LANES
A_TAIL = 32
B_TAIL = 8
ROW_TILE = 512
CONV_ROWS = 128
SAMPLE_SSD_BLOCK = 8
SAMPLE_CONV_LANES = 256
VMEM_LIMIT = 56 * 1024 * 1024
NEG_BIG = -1e30

ACT_DTYPE = jnp.float32
MXU_DTYPE = jnp.bfloat16


def _dot(a, b):
    return jnp.dot(a.astype(MXU_DTYPE), b.astype(MXU_DTYPE), preferred_element_type=jnp.float32)


def _dot_nt(a, b):
    return lax.dot_general(a.astype(MXU_DTYPE), b.astype(MXU_DTYPE), (((1,), (1,)), ((), ())),
                           preferred_element_type=jnp.float32)


def _split2(v):
    hi = v.astype(MXU_DTYPE)
    lo = (v - hi.astype(jnp.float32)).astype(MXU_DTYPE)
    return hi, lo


def _dot_split_lhs(v, m):
    hi, lo = _split2(v)
    return (jnp.dot(hi, m, preferred_element_type=jnp.float32)
            + jnp.dot(lo, m, preferred_element_type=jnp.float32))


def _dot_split_rhs(m, v):
    hi = v.astype(MXU_DTYPE)
    r1 = v - hi.astype(jnp.float32)
    mid = r1.astype(MXU_DTYPE)
    lo = (r1 - mid.astype(jnp.float32)).astype(MXU_DTYPE)
    return (jnp.dot(m, hi, preferred_element_type=jnp.float32)
            + jnp.dot(m, mid, preferred_element_type=jnp.float32)
            + jnp.dot(m, lo, preferred_element_type=jnp.float32))


def _rms_norm(x, g):
    return x * lax.rsqrt(jnp.mean(x * x, axis=-1, keepdims=True) + NORM_EPS) * g


def _silu(x):
    return x * jax.nn.sigmoid(x)


def _softplus(x):
    return jnp.maximum(x, 0.0) + jnp.log1p(jnp.exp(-jnp.abs(x)))


def _layer_norm_silu(x, g, b):
    mu = jnp.mean(x, axis=-1, keepdims=True)
    xc = x - mu
    var = jnp.mean(xc * xc, axis=-1, keepdims=True)
    return _silu(xc * lax.rsqrt(var + NORM_EPS) * g + b)


def _gate_group_norm(y, z, g):
    y = y * _silu(z)
    parts = []
    for i in range(SSM_GROUPS):
        yg = y[:, i * GROUP_WIDTH:(i + 1) * GROUP_WIDTH]
        parts.append(yg * lax.rsqrt(jnp.mean(yg * yg, axis=-1, keepdims=True) + NORM_EPS))
    return jnp.concatenate(parts, axis=-1) * g


def _causal_conv_tile(ext_ref, w_ref, lanes, row0, n_out, tail, width):
    n_rows = tail + n_out
    x = ext_ref[row0:row0 + n_rows, lanes]
    first = tail - (width - 1)
    acc = jnp.zeros((n_out, LANES), jnp.float32)
    for r in range(SUBLANES):
        taps = [k for k in range(width) if (first + k) % SUBLANES == r]
        if not taps:
            continue
        xr = x if r == 0 else pltpu.roll(x, n_rows - r, 0)
        for k in taps:
            base = first + k - r
            acc = acc + xr[base:base + n_out, :] * w_ref[k:k + 1, lanes]
    return acc


def _const_spec(shape):
    nd = len(shape)
    return pl.BlockSpec(shape, lambda *_: (0,) * nd, pipeline_mode=pl.Buffered(1))


def _full_spec(shape):
    nd = len(shape)
    return pl.BlockSpec(shape, lambda *_: (0,) * nd)


def _params(*semantics):
    return pltpu.CompilerParams(dimension_semantics=semantics, vmem_limit_bytes=VMEM_LIMIT)


def _ffn_kernel(*refs, with_outproj, with_final_norm):
    refs = list(refs)
    x_ref = refs.pop(0)
    if with_outproj:
        ua_ref, lng_ref, lnb_ref, ys_ref, wo_ref = refs[:5]
        refs = refs[5:]
    g_ref, wg_ref, wu_ref, wd_ref = refs[:4]
    refs = refs[4:]
    if with_final_norm:
        fg_ref = refs.pop(0)
    o_ref = refs.pop(0)

    x = x_ref[...]
    if with_outproj:
        ya = _layer_norm_silu(ua_ref[...].astype(jnp.float32), lng_ref[...], lnb_ref[...])
        x = x + _dot(ya, wo_ref[0:D_CONV, :]) + _dot(ys_ref[...], wo_ref[D_CONV:D_CONV + D_SSM, :])
    xn = _rms_norm(x, g_ref[...]).astype(MXU_DTYPE)
    gate = jnp.dot(xn, wg_ref[...], preferred_element_type=jnp.float32)
    up = jnp.dot(xn, wu_ref[...], preferred_element_type=jnp.float32)
    h = (_silu(gate) * up).astype(MXU_DTYPE)
    y = x + FFN_RES_WEIGHT * jnp.dot(h, wd_ref[...], preferred_element_type=jnp.float32)
    if with_final_norm:
        y = _rms_norm(y, fg_ref[...])
    o_ref[...] = y


def _ffn(x, norm_g, wg, wu, wd, ua=None, ln=None, ys=None, wo=None, final_g=None):
    m = x.shape[0]
    tm = min(m, ROW_TILE)
    with_outproj = ua is not None
    with_final_norm = final_g is not None
    row = lambda i: (i, 0)
    args, specs = [x], [pl.BlockSpec((tm, D_MODEL), row)]
    if with_outproj:
        args += [ua, ln[0], ln[1], ys, wo]
        specs += [pl.BlockSpec((tm, D_CONV), row), _const_spec(ln[0].shape), _const_spec(ln[1].shape),
                  pl.BlockSpec((tm, D_SSM), row), _const_spec(wo.shape)]
    args += [norm_g, wg, wu, wd]
    specs += [_const_spec(norm_g.shape), _const_spec(wg.shape), _const_spec(wu.shape), _const_spec(wd.shape)]
    if with_final_norm:
        args.append(final_g)
        specs.append(_const_spec(final_g.shape))
    return pl.pallas_call(
        functools.partial(_ffn_kernel, with_outproj=with_outproj, with_final_norm=with_final_norm),
        grid=(m // tm,),
        in_specs=specs,
        out_specs=pl.BlockSpec((tm, D_MODEL), row),
        out_shape=jax.ShapeDtypeStruct((m, D_MODEL), jnp.float32),
        compiler_params=_params("arbitrary"),
        name="ffn_out" if with_outproj else "ffn_in",
    )(*args)


def _project(x_ref, g_ref, w_ref, wdt_ref, z_ref, xbc_ref, dt_ref):
    xn = _rms_norm(x_ref[...], g_ref[...]).astype(MXU_DTYPE)

    def proj(lo, width):
        return jnp.dot(xn, w_ref[:, lo:lo + width], preferred_element_type=jnp.float32)

    u = proj(0, D_CONV) * jax.nn.sigmoid(proj(D_CONV, D_CONV))
    z_ref[...] = proj(2 * D_CONV, D_SSM).astype(z_ref.dtype)
    xbc_ref[...] = proj(2 * D_CONV + D_SSM, D_XBC).astype(xbc_ref.dtype)
    dt_ref[...] = jnp.dot(xn, wdt_ref[...], preferred_element_type=jnp.float32)
    return u


def _inproj_kernel(x_ref, g_ref, w_ref, wdt_ref, u_ref, z_ref, xbc_ref, dt_ref):
    u_ref[...] = _project(x_ref, g_ref, w_ref, wdt_ref, z_ref, xbc_ref, dt_ref)


def _inproj_conv_kernel(x_ref, g_ref, w_ref, wdt_ref, cwa_ref, cba_ref,
                        ua_ref, tail_ref, z_ref, xbc_ref, dt_ref, ext_ref, *, tiles_per_seq):
    tm = x_ref.shape[0]

    @pl.when(pl.program_id(0) % tiles_per_seq == 0)
    def _():
        ext_ref[0:A_TAIL, :] = jnp.zeros((A_TAIL, D_CONV), jnp.float32)

    ext_ref[A_TAIL:A_TAIL + tm, :] = _project(x_ref, g_ref, w_ref, wdt_ref, z_ref, xbc_ref, dt_ref)
    for c in range(tm // CONV_ROWS):
        for j in range(D_CONV // LANES):
            lanes = slice(j * LANES, (j + 1) * LANES)
            ua_ref[c * CONV_ROWS:(c + 1) * CONV_ROWS, lanes] = (
                _causal_conv_tile(ext_ref, cwa_ref, lanes, c * CONV_ROWS, CONV_ROWS, A_TAIL, CONV_A_WIDTH)
                + cba_ref[:, lanes]).astype(ua_ref.dtype)
    tail = ext_ref[tm:tm + A_TAIL, :]
    tail_ref[0] = tail
    ext_ref[0:A_TAIL, :] = tail


def _inproj(x, norm_g, w_main, w_dt):
    m = x.shape[0]
    tm = min(m, ROW_TILE)
    row = lambda i: (i, 0)
    widths = (D_CONV, D_SSM, D_XBC, DT_PAD)
    dtypes = (jnp.float32, ACT_DTYPE, ACT_DTYPE, jnp.float32)
    return pl.pallas_call(
        _inproj_kernel,
        grid=(m // tm,),
        in_specs=[pl.BlockSpec((tm, D_MODEL), row), _const_spec(norm_g.shape), _const_spec(w_main.shape),
                  _const_spec(w_dt.shape)],
        out_specs=[pl.BlockSpec((tm, w), row) for w in widths],
        out_shape=[jax.ShapeDtypeStruct((m, w), d) for w, d in zip(widths, dtypes)],
        compiler_params=_params("arbitrary"),
        name="inproj",
    )(x, norm_g, w_main, w_dt)


def _inproj_conv(x, norm_g, w_main, w_dt, conv, seq_len):
    m = x.shape[0]
    tm = min(seq_len, ROW_TILE)
    tiles_per_seq = seq_len // tm
    f32 = jnp.float32
    row = lambda i: (i, 0)
    return pl.pallas_call(
        functools.partial(_inproj_conv_kernel, tiles_per_seq=tiles_per_seq),
        grid=(m // tm,),
        in_specs=[pl.BlockSpec((tm, D_MODEL), row), _const_spec(norm_g.shape), _const_spec(w_main.shape),
                  _const_spec(w_dt.shape)] + [_const_spec(a.shape) for a in conv],
        out_specs=[pl.BlockSpec((tm, D_CONV), row),
                   pl.BlockSpec((1, A_TAIL, D_CONV), lambda i: (i // tiles_per_seq, 0, 0))]
                  + [pl.BlockSpec((tm, w), row) for w in (D_SSM, D_XBC, DT_PAD)],
        out_shape=[jax.ShapeDtypeStruct((m, D_CONV), ACT_DTYPE),
                   jax.ShapeDtypeStruct((m // seq_len, A_TAIL, D_CONV), f32),
                   jax.ShapeDtypeStruct((m, D_SSM), ACT_DTYPE), jax.ShapeDtypeStruct((m, D_XBC), ACT_DTYPE),
                   jax.ShapeDtypeStruct((m, DT_PAD), f32)],
        scratch_shapes=[pltpu.VMEM((A_TAIL + tm, D_CONV), f32)],
        compiler_params=_params("arbitrary"),
        name="inproj_conv",
    )(x, norm_g, w_main, w_dt, *conv)


def _ssd_kernel(z_ref, xbc_ref, dt_ref,
                cwb_ref, cbb_ref, dtb_ref, a_ref, dexp_ref, nrm_ref, e_ref,
                ys_ref, ssm_ref,
                extb_ref, ht_ref, xc_ref, y_ref):
    c = pl.program_id(1)
    f32 = jnp.float32

    @pl.when(c == 0)
    def _():
        extb_ref[0:B_TAIL, :] = jnp.zeros((B_TAIL, D_XBC), f32)
        ht_ref[...] = jnp.zeros_like(ht_ref)

    extb_ref[B_TAIL:B_TAIL + CHUNK, :] = xbc_ref[0].astype(f32)
    for j in range(D_XBC // LANES):
        lanes = slice(j * LANES, (j + 1) * LANES)
        xc_ref[:, lanes] = _silu(_causal_conv_tile(extb_ref, cwb_ref, lanes, 0, CHUNK, B_TAIL, SSM_CONV_WIDTH)
                                 + cbb_ref[:, lanes])
    extb_ref[0:B_TAIL, :] = extb_ref[CHUNK:CHUNK + B_TAIL, :]

    dt = _softplus(dt_ref[0] + dtb_ref[...])
    a = dt * a_ref[...]
    rows = lax.broadcasted_iota(jnp.int32, (CHUNK, CHUNK), 0)
    cols = lax.broadcasted_iota(jnp.int32, (CHUNK, CHUNK), 1)
    causal = rows >= cols
    tri = causal.astype(MXU_DTYPE)
    a_cs = _dot_split_rhs(tri, a)
    a_cs_t = a_cs.T
    dt_t = dt.T
    ea = jnp.exp(a_cs)
    w_end = jnp.exp(a_cs[CHUNK - 1:CHUNK, :] - a_cs) * dt
    e_mat = e_ref[...]
    w_exp = _dot_split_lhs(w_end, e_mat)
    ea_exp = _dot_split_lhs(ea, e_mat)
    xs = xc_ref[:, 0:D_SSM]
    xs_b = xs.astype(MXU_DTYPE)
    xw_b = (xs * w_exp).astype(MXU_DTYPE)
    lane = lax.broadcasted_iota(jnp.int32, (CHUNK, LANES), 1)
    gw = GROUP_WIDTH
    for g in range(SSM_GROUPS):
        b_g = xc_ref[:, D_SSM + g * D_STATE:D_SSM + (g + 1) * D_STATE]
        c_g = xc_ref[:, D_SSM + D_BC + g * D_STATE:D_SSM + D_BC + (g + 1) * D_STATE]
        c_gb = c_g.astype(MXU_DTYPE)
        cb = _dot_nt(c_gb, b_g)
        ht_g = ht_ref[:, g * gw:(g + 1) * gw]
        y_off = _dot(c_gb, ht_g) * ea_exp[:, g * gw:(g + 1) * gw]
        for hp in range(HEADS_PER_GROUP // 2):
            res = []
            for e in range(2):
                h = g * HEADS_PER_GROUP + 2 * hp + e
                seg = a_cs[:, h:h + 1] - a_cs_t[h:h + 1, :]
                l_mat = jnp.exp(jnp.where(causal, seg, NEG_BIG))
                gmat = (cb * l_mat * dt_t[h:h + 1, :]).astype(MXU_DTYPE)
                lo = g * gw + hp * LANES
                res.append(jnp.dot(gmat, xs_b[:, lo:lo + LANES], preferred_element_type=f32))
            y_pair = jnp.where(lane < SSM_HEAD_DIM, res[0], res[1])
            y_ref[:, lo:lo + LANES] = y_pair + y_off[:, hp * LANES:(hp + 1) * LANES]
        s_loc = _dot(b_g.T, xw_b[:, g * gw:(g + 1) * gw])
        ht_ref[:, g * gw:(g + 1) * gw] = ht_g * ea_exp[CHUNK - 1:CHUNK, g * gw:(g + 1) * gw] + s_loc

    y = y_ref[...] + dexp_ref[...] * xs
    ys_ref[0] = _gate_group_norm(y, z_ref[0].astype(f32), nrm_ref[...]).astype(ys_ref.dtype)

    @pl.when(c == pl.num_programs(1) - 1)
    def _():
        for j in range(D_SSM // LANES):
            ssm_ref[0, j * LANES:(j + 1) * LANES, :] = ht_ref[:, j * LANES:(j + 1) * LANES].T


def _ssd(z, xbc, dt_raw, p):
    b, l, _ = z.shape
    nc = l // CHUNK
    tok = lambda i, c: (i, c, 0)
    consts = [p["cwb"], p["cbb"], p["dtb"], p["a"], p["dexp"], p["nrm"], p["e"]]
    f32 = jnp.float32
    return pl.pallas_call(
        _ssd_kernel,
        grid=(b, nc),
        in_specs=[pl.BlockSpec((1, CHUNK, D_SSM), tok), pl.BlockSpec((1, CHUNK, D_XBC), tok),
                  pl.BlockSpec((1, CHUNK, DT_PAD), tok)] + [_const_spec(a.shape) for a in consts],
        out_specs=[pl.BlockSpec((1, CHUNK, D_SSM), tok),
                   pl.BlockSpec((1, D_SSM, D_STATE), lambda i, c: (i, 0, 0))],
        out_shape=[jax.ShapeDtypeStruct((b, l, D_SSM), ACT_DTYPE),
                   jax.ShapeDtypeStruct((b, D_SSM, D_STATE), f32)],
        scratch_shapes=[pltpu.VMEM((B_TAIL + CHUNK, D_XBC), f32),
                        pltpu.VMEM((D_STATE, D_SSM), f32),
                        pltpu.VMEM((CHUNK, D_XBC), f32),
                        pltpu.VMEM((CHUNK, D_SSM), f32)],
        compiler_params=_params("arbitrary", "arbitrary"),
        name="ssd",
    )(z, xbc, dt_raw, *consts)


def _sample_conv_a_kernel(st_ref, u_ref, w_ref, cba_ref, new_ref, ua_ref):
    hist = CONV_A_WIDTH - 1
    for j in range(u_ref.shape[1] // LANES):
        lanes = slice(j * LANES, (j + 1) * LANES)
        u = u_ref[:, lanes]
        acc = u * w_ref[hist:hist + 1, lanes] + cba_ref[:, lanes]
        for k in range(hist):
            row = st_ref[k, :, lanes]
            acc = acc + row * w_ref[k:k + 1, lanes]
            if k >= 1:
                new_ref[k - 1, :, lanes] = row
        new_ref[hist - 1, :, lanes] = u
        ua_ref[:, lanes] = acc


def _sample_conv_a(state_k, u, p):
    hist, s, _ = state_k.shape
    wb = SAMPLE_CONV_LANES
    f32 = jnp.float32
    blk3 = pl.BlockSpec((hist, s, wb), lambda j: (0, 0, j))
    blk2 = pl.BlockSpec((s, wb), lambda j: (0, j))
    return pl.pallas_call(
        _sample_conv_a_kernel,
        grid=(D_CONV // wb,),
        in_specs=[blk3, blk2, pl.BlockSpec((A_TAIL, wb), lambda j: (0, j)), pl.BlockSpec((1, wb), lambda j: (0, j))],
        out_specs=[blk3, blk2],
        out_shape=[jax.ShapeDtypeStruct((hist, s, D_CONV), f32), jax.ShapeDtypeStruct((s, D_CONV), f32)],
        compiler_params=_params("arbitrary"),
        name="sample_conv_a",
    )(state_k, u, p["cwa"], p["cba"])


def _sample_conv_b_kernel(st_ref, xbc_ref, dt_ref, cwb_ref, cbb_ref, dtb_ref, a_ref, e_ref,
                          new_ref, xs_ref, xt_ref, dat_ref, b_ref, ct_ref):
    hist = SSM_CONV_WIDTH - 1
    xbc = xbc_ref[...].astype(jnp.float32)
    acc = xbc * cwb_ref[hist:hist + 1, :] + cbb_ref[...]
    for k in range(hist):
        acc = acc + st_ref[k] * cwb_ref[k:k + 1, :]
    for k in range(hist - 1):
        new_ref[k] = st_ref[k + 1]
    new_ref[hist - 1] = xbc
    xc = _silu(acc)
    xs = xc[:, 0:D_SSM]
    dt = _softplus(dt_ref[...] + dtb_ref[...])
    da = jnp.exp(dt * a_ref[...])
    xs_ref[...] = xs
    x_dt = xs * _dot_split_lhs(dt, e_ref[...])
    for j in range(D_SSM // LANES):
        xt_ref[j * LANES:(j + 1) * LANES, :] = x_dt[:, j * LANES:(j + 1) * LANES].T.astype(xt_ref.dtype)
    dat_ref[...] = da.T[0:SSM_HEADS, :]
    b_ref[...] = xc[:, D_SSM:D_SSM + D_BC]
    for g in range(SSM_GROUPS):
        lo = D_SSM + D_BC + g * D_STATE
        ct_ref[g * D_STATE:(g + 1) * D_STATE, :] = xc[:, lo:lo + D_STATE].T.astype(ct_ref.dtype)


def _sample_conv_b(state_k, xbc, dt_raw, p):
    hist, s, _ = state_k.shape
    f32 = jnp.float32
    args = [state_k, xbc, dt_raw, p["cwb"], p["cbb"], p["dtb"], p["a"], p["e"]]
    shapes = [((hist, s, D_XBC), f32), ((s, D_SSM), f32), ((D_SSM, s), MXU_DTYPE), ((SSM_HEADS, s), f32),
              ((s, D_BC), f32), ((D_BC, s), MXU_DTYPE)]
    return pl.pallas_call(
        _sample_conv_b_kernel,
        grid=(1,),
        in_specs=[_const_spec(a.shape) for a in args],
        out_specs=[_full_spec(sh) for sh, _ in shapes],
        out_shape=[jax.ShapeDtypeStruct(sh, d) for sh, d in shapes],
        compiler_params=_params("arbitrary"),
        name="sample_conv_b",
    )(*args)


def _sample_ssd_kernel(dat_ref, st_ref, xt_ref, b_ref, ct_ref, new_ref, yt_ref):
    i = pl.program_id(0)
    bs = st_ref.shape[0]
    n_seq = yt_ref.shape[1]
    f32 = jnp.float32

    @pl.when(i == 0)
    def _():
        yt_ref[...] = jnp.zeros_like(yt_ref)

    row_id = lax.broadcasted_iota(jnp.int32, (n_seq, D_STATE), 0)
    col_id = lax.broadcasted_iota(jnp.int32, (GROUP_WIDTH, n_seq), 1)

    def body(sl, carry):
        s = i * bs + sl
        for g in range(SSM_GROUPS):
            rows = slice(g * GROUP_WIDTH, (g + 1) * GROUP_WIDTH)
            b_sel = jnp.where(row_id == s, b_ref[:, g * D_STATE:(g + 1) * D_STATE], 0.0).astype(MXU_DTYPE)
            outer = jnp.dot(xt_ref[rows, :], b_sel, preferred_element_type=f32)
            h_new = []
            for hh in range(HEADS_PER_GROUP):
                h = g * HEADS_PER_GROUP + hh
                hn = st_ref[sl, h] * dat_ref[h, s] + outer[hh * SSM_HEAD_DIM:(hh + 1) * SSM_HEAD_DIM, :]
                new_ref[sl, h] = hn
                h_new.append(hn.astype(MXU_DTYPE))
            y_all = jnp.dot(jnp.concatenate(h_new, axis=0), ct_ref[g * D_STATE:(g + 1) * D_STATE, :],
                            preferred_element_type=f32)
            yt_ref[rows, :] = jnp.where(col_id == s, y_all, yt_ref[rows, :])
        return carry

    lax.fori_loop(0, bs, body, 0, unroll=2)


def _sample_ssd(state, xt, dat, b, ct):
    s = state.shape[0]
    bs = SAMPLE_SSD_BLOCK
    blk = pl.BlockSpec((bs, SSM_HEADS, SSM_HEAD_DIM, D_STATE), lambda i: (i, 0, 0, 0))
    return pl.pallas_call(
        _sample_ssd_kernel,
        grid=(s // bs,),
        in_specs=[pl.BlockSpec(memory_space=pltpu.SMEM), blk,
                  _const_spec(xt.shape), _const_spec(b.shape), _const_spec(ct.shape)],
        out_specs=[blk, _full_spec((D_SSM, s))],
        out_shape=[jax.ShapeDtypeStruct(state.shape, jnp.float32), jax.ShapeDtypeStruct((D_SSM, s), jnp.float32)],
        compiler_params=_params("arbitrary"),
        name="sample_ssd",
    )(dat, state, xt, b, ct)


def _sample_gate_kernel(yt_ref, xs_ref, z_ref, dexp_ref, nrm_ref, ys_ref):
    y = jnp.concatenate([yt_ref[j * LANES:(j + 1) * LANES, :].T for j in range(D_SSM // LANES)], axis=-1)
    y = y + dexp_ref[...] * xs_ref[...]
    ys_ref[...] = _gate_group_norm(y, z_ref[...].astype(jnp.float32), nrm_ref[...]).astype(ys_ref.dtype)


def _sample_gate(yt, xs, z, p):
    s = xs.shape[0]
    args = [yt, xs, z, p["dexp"], p["nrm"]]
    return pl.pallas_call(
        _sample_gate_kernel,
        grid=(1,),
        in_specs=[_const_spec(a.shape) for a in args],
        out_specs=_full_spec((s, D_SSM)),
        out_shape=jax.ShapeDtypeStruct((s, D_SSM), ACT_DTYPE),
        compiler_params=_params("arbitrary"),
        name="sample_gate",
    )(*args)


def _prep_params(i, conv_dw_w, conv_dw_b, conv_ln_g, conv_ln_b, ssm_conv_w, ssm_conv_b, ssm_dt_bias, ssm_a_log,
                 ssm_d, ssm_norm):
    f32 = jnp.float32
    row = lambda v: v.reshape(1, -1).astype(f32)
    pad_heads = lambda v: jnp.pad(v.astype(f32), (0, DT_PAD - SSM_HEADS)).reshape(1, DT_PAD)
    head_of_channel = jnp.arange(D_SSM, dtype=jnp.int32) // SSM_HEAD_DIM
    expand = (jnp.arange(DT_PAD, dtype=jnp.int32)[:, None] == head_of_channel[None, :]).astype(MXU_DTYPE)
    return {
        "cwa": jnp.pad(conv_dw_w[i].astype(f32), ((0, A_TAIL - CONV_A_WIDTH), (0, 0))),
        "cba": row(conv_dw_b[i]), "lng": row(conv_ln_g[i]), "lnb": row(conv_ln_b[i]),
        "cwb": jnp.pad(ssm_conv_w[i].astype(f32), ((0, B_TAIL - SSM_CONV_WIDTH), (0, 0))),
        "cbb": row(ssm_conv_b[i]),
        "dtb": pad_heads(ssm_dt_bias[i]),
        "a": pad_heads(-jnp.exp(ssm_a_log[i].astype(f32))),
        "dexp": jnp.repeat(ssm_d[i].astype(f32), SSM_HEAD_DIM).reshape(1, D_SSM),
        "nrm": row(ssm_norm[i]),
        "e": expand,
    }


def kernel(x_prompt, x_sample, state_conv_a, state_conv_b, state_ssm, ffn1_norm, ffn1_w_gate, ffn1_w_up, ffn1_w_down, mix_norm, w_in, conv_dw_w, conv_dw_b, conv_ln_g, conv_ln_b, ssm_conv_w, ssm_conv_b, ssm_dt_bias, ssm_a_log, ssm_d, ssm_norm, w_out, ffn2_norm, ffn2_w_gate, ffn2_w_up, ffn2_w_down, final_norm):
    depth = ffn1_norm.shape[0]
    bp, lp, _ = x_prompt.shape
    bs = x_sample.shape[0]
    f32 = jnp.float32
    row = lambda v: v.reshape(1, -1).astype(f32)
    w16 = lambda w: w.astype(MXU_DTYPE)

    yp = x_prompt.reshape(bp * lp, D_MODEL)
    ysm = x_sample.reshape(bs, D_MODEL)
    outs = [[] for _ in range(6)]
    for i in range(depth):
        p = _prep_params(i, conv_dw_w, conv_dw_b, conv_ln_g, conv_ln_b, ssm_conv_w, ssm_conv_b, ssm_dt_bias,
                         ssm_a_log, ssm_d, ssm_norm)
        n_main = 2 * D_CONV + D_SSM + D_XBC
        w_main = w16(w_in[i][:, :n_main])
        w_dt = w16(jnp.pad(w_in[i][:, n_main:], ((0, 0), (0, DT_PAD - SSM_HEADS))))
        ln = (p["lng"], p["lnb"])
        f1 = (row(ffn1_norm[i]), w16(ffn1_w_gate[i]), w16(ffn1_w_up[i]), w16(ffn1_w_down[i]))
        f2 = (row(ffn2_norm[i]), w16(ffn2_w_gate[i]), w16(ffn2_w_up[i]), w16(ffn2_w_down[i]))
        wo = w16(w_out[i])
        fin = row(final_norm) if i == depth - 1 else None

        x1 = _ffn(yp, *f1)
        ua, a_tail, z, xbc, dt_raw = _inproj_conv(x1, row(mix_norm[i]), w_main, w_dt, (p["cwa"], p["cba"]), lp)
        xbc3 = xbc.reshape(bp, lp, D_XBC)
        ys, ssm_t = _ssd(z.reshape(bp, lp, D_SSM), xbc3, dt_raw.reshape(bp, lp, DT_PAD), p)
        yp = _ffn(x1, *f2, ua=ua, ln=ln, ys=ys.reshape(bp * lp, D_SSM), wo=wo, final_g=fin)
        outs[0].append(a_tail[:, A_TAIL - (CONV_A_WIDTH - 1):, :].astype(x_prompt.dtype))
        outs[1].append(xbc3[:, lp - (SSM_CONV_WIDTH - 1):, :].astype(x_prompt.dtype))
        outs[2].append(ssm_t.reshape(bp, SSM_HEADS, SSM_HEAD_DIM, D_STATE).astype(state_ssm.dtype))

        s1 = _ffn(ysm, *f1)
        su, sz, sxbc, sdt = _inproj(s1, row(mix_norm[i]), w_main, w_dt)
        tap_major = lambda st: jnp.transpose(st.astype(f32), (1, 0, 2))
        new_a, sua = _sample_conv_a(tap_major(state_conv_a[i]), su, p)
        new_b, xs, xt, dat, b_mat, ct = _sample_conv_b(tap_major(state_conv_b[i]), sxbc, sdt, p)
        new_s, yt = _sample_ssd(state_ssm[i].astype(f32), xt, dat, b_mat, ct)
        sys_ = _sample_gate(yt, xs, sz, p)
        ysm = _ffn(s1, *f2, ua=sua, ln=ln, ys=sys_, wo=wo, final_g=fin)
        outs[3].append(jnp.transpose(new_a, (1, 0, 2)).astype(x_sample.dtype))
        outs[4].append(jnp.transpose(new_b, (1, 0, 2)).astype(x_sample.dtype))
        outs[5].append(new_s.astype(state_ssm.dtype))

    return (yp.reshape(bp, lp, D_MODEL), ysm.reshape(bs, 1, D_MODEL)) + tuple(jnp.stack(o) for o in outs)
```

```python
import functools

import jax
import jax.numpy as jnp
from jax import lax
from jax.experimental import pallas as pl
from jax.experimental.pallas import tpu as pltpu

D_MODEL = 1024
D_FF = 2816
D_CONV = 1024
CONV_A_WIDTH = 31
D_SSM = 1024
SSM_HEAD_DIM = 64
SSM_HEADS = 16
SSM_GROUPS = 2
HEADS_PER_GROUP = SSM_HEADS // SSM_GROUPS
GROUP_WIDTH = HEADS_PER_GROUP * SSM_HEAD_DIM
D_STATE = 128
SSM_CONV_WIDTH = 4
CHUNK = 128
D_XBC = D_SSM + 2 * SSM_GROUPS * D_STATE
D_BC = SSM_GROUPS * D_STATE
FFN_RES_WEIGHT = 0.5
NORM_EPS = 1e-5

LANES = 128
SUBLANES = 8
DT_PAD = LANES
A_TAIL = 32
B_TAIL = 8
ROW_TILE = 512
CONV_ROWS = 128
SSD_CHUNKS_PER_STEP = 8
SAMPLE_SSD_BLOCK = 8
SAMPLE_CONV_LANES = 256
VMEM_LIMIT = 56 * 1024 * 1024
NEG_BIG = -1e30

ACT_DTYPE = jnp.float32
MXU_DTYPE = jnp.bfloat16


def _dot(a, b):
    return jnp.dot(a.astype(MXU_DTYPE), b.astype(MXU_DTYPE), preferred_element_type=jnp.float32)


def _dot_nt(a, b):
    return lax.dot_general(a.astype(MXU_DTYPE), b.astype(MXU_DTYPE), (((1,), (1,)), ((), ())),
                           preferred_element_type=jnp.float32)


def _split2(v):
    hi = v.astype(MXU_DTYPE)
    lo = (v - hi.astype(jnp.float32)).astype(MXU_DTYPE)
    return hi, lo


def _dot_split_lhs(v, m):
    hi, lo = _split2(v)
    return (jnp.dot(hi, m, preferred_element_type=jnp.float32)
            + jnp.dot(lo, m, preferred_element_type=jnp.float32))


def _dot_split_rhs(m, v):
    hi = v.astype(MXU_DTYPE)
    r1 = v - hi.astype(jnp.float32)
    mid = r1.astype(MXU_DTYPE)
    lo = (r1 - mid.astype(jnp.float32)).astype(MXU_DTYPE)
    return (jnp.dot(m, hi, preferred_element_type=jnp.float32)
            + jnp.dot(m, mid, preferred_element_type=jnp.float32)
            + jnp.dot(m, lo, preferred_element_type=jnp.float32))


def _rms_norm(x, g):
    return x * lax.rsqrt(jnp.mean(x * x, axis=-1, keepdims=True) + NORM_EPS) * g


def _silu(x):
    return x * jax.nn.sigmoid(x)


def _softplus(x):
    return jnp.maximum(x, 0.0) + jnp.log1p(jnp.exp(-jnp.abs(x)))


def _layer_norm_silu(x, g, b):
    mu = jnp.mean(x, axis=-1, keepdims=True)
    xc = x - mu
    var = jnp.mean(xc * xc, axis=-1, keepdims=True)
    return _silu(xc * lax.rsqrt(var + NORM_EPS) * g + b)


def _gate_group_norm(y, z, g):
    y = y * _silu(z)
    parts = []
    for i in range(SSM_GROUPS):
        yg = y[:, i * GROUP_WIDTH:(i + 1) * GROUP_WIDTH]
        parts.append(yg * lax.rsqrt(jnp.mean(yg * yg, axis=-1, keepdims=True) + NORM_EPS))
    return jnp.concatenate(parts, axis=-1) * g


def _causal_conv_tile(ext_ref, w_ref, lanes, row0, n_out, tail, width):
    n_rows = tail + n_out
    x = ext_ref[row0:row0 + n_rows, lanes]
    first = tail - (width - 1)
    acc = jnp.zeros((n_out, LANES), jnp.float32)
    for r in range(SUBLANES):
        taps = [k for k in range(width) if (first + k) % SUBLANES == r]
        if not taps:
            continue
        xr = x if r == 0 else pltpu.roll(x, n_rows - r, 0)
        for k in taps:
            base = first + k - r
            acc = acc + xr[base:base + n_out, :] * w_ref[k:k + 1, lanes]
    return acc


def _const_spec(shape):
    nd = len(shape)
    return pl.BlockSpec(shape, lambda *_: (0,) * nd, pipeline_mode=pl.Buffered(1))


def _full_spec(shape):
    nd = len(shape)
    return pl.BlockSpec(shape, lambda *_: (0,) * nd)


def _params(*semantics):
    return pltpu.CompilerParams(dimension_semantics=semantics, vmem_limit_bytes=VMEM_LIMIT)


def _ffn_kernel(*refs, with_outproj, with_final_norm):
    refs = list(refs)
    x_ref = refs.pop(0)
    if with_outproj:
        ya_ref, ys_ref, wo_ref = refs[:3]
        refs = refs[3:]
    g_ref, wg_ref, wu_ref, wd_ref = refs[:4]
    refs = refs[4:]
    if with_final_norm:
        fg_ref = refs.pop(0)
    o_ref = refs.pop(0)

    x = x_ref[...]
    if with_outproj:
        x = x + _dot(ya_ref[...], wo_ref[0:D_CONV, :]) + _dot(ys_ref[...], wo_ref[D_CONV:D_CONV + D_SSM, :])
    xn = _rms_norm(x, g_ref[...]).astype(MXU_DTYPE)
    gate = jnp.dot(xn, wg_ref[...], preferred_element_type=jnp.float32)
    up = jnp.dot(xn, wu_ref[...], preferred_element_type=jnp.float32)
    h = (_silu(gate) * up).astype(MXU_DTYPE)
    y = x + FFN_RES_WEIGHT * jnp.dot(h, wd_ref[...], preferred_element_type=jnp.float32)
    if with_final_norm:
        y = _rms_norm(y, fg_ref[...])
    o_ref[...] = y


def _ffn(x, norm_g, wg, wu, wd, ya=None, ys=None, wo=None, final_g=None):
    m = x.shape[0]
    tm = min(m, ROW_TILE)
    with_outproj = ya is not None
    with_final_norm = final_g is not None
    row = lambda i: (i, 0)
    args, specs = [x], [pl.BlockSpec((tm, D_MODEL), row)]
    if with_outproj:
        args += [ya, ys, wo]
        specs += [pl.BlockSpec((tm, D_CONV), row), pl.BlockSpec((tm, D_SSM), row), _const_spec(wo.shape)]
    args += [norm_g, wg, wu, wd]
    specs += [_const_spec(norm_g.shape), _const_spec(wg.shape), _const_spec(wu.shape), _const_spec(wd.shape)]
    if with_final_norm:
        args.append(final_g)
        specs.append(_const_spec(final_g.shape))
    return pl.pallas_call(
        functools.partial(_ffn_kernel, with_outproj=with_outproj, with_final_norm=with_final_norm),
        grid=(m // tm,),
        in_specs=specs,
        out_specs=pl.BlockSpec((tm, D_MODEL), row),
        out_shape=jax.ShapeDtypeStruct((m, D_MODEL), jnp.float32),
        compiler_params=_params("arbitrary"),
        name="ffn_out" if with_outproj else "ffn_in",
    )(*args)


def _project(x_ref, g_ref, w_ref, wdt_ref, z_ref, xbc_ref, dt_ref):
    xn = _rms_norm(x_ref[...], g_ref[...]).astype(MXU_DTYPE)

    def proj(lo, width):
        return jnp.dot(xn, w_ref[:, lo:lo + width], preferred_element_type=jnp.float32)

    u = proj(0, D_CONV) * jax.nn.sigmoid(proj(D_CONV, D_CONV))
    z_ref[...] = proj(2 * D_CONV, D_SSM).astype(z_ref.dtype)
    xbc_ref[...] = proj(2 * D_CONV + D_SSM, D_XBC).astype(xbc_ref.dtype)
    dt_ref[...] = jnp.dot(xn, wdt_ref[...], preferred_element_type=jnp.float32)
    return u


def _inproj_kernel(x_ref, g_ref, w_ref, wdt_ref, u_ref, z_ref, xbc_ref, dt_ref):
    u_ref[...] = _project(x_ref, g_ref, w_ref, wdt_ref, z_ref, xbc_ref, dt_ref)


def _inproj_conv_kernel(x_ref, g_ref, w_ref, wdt_ref, cwa_ref, cba_ref, lng_ref, lnb_ref,
                        ya_ref, tail_ref, z_ref, xbc_ref, dt_ref, ext_ref, ua_ref, *, tiles_per_seq):
    tm = x_ref.shape[0]

    @pl.when(pl.program_id(0) % tiles_per_seq == 0)
    def _():
        ext_ref[0:A_TAIL, :] = jnp.zeros((A_TAIL, D_CONV), jnp.float32)

    ext_ref[A_TAIL:A_TAIL + tm, :] = _project(x_ref, g_ref, w_ref, wdt_ref, z_ref, xbc_ref, dt_ref)
    for c in range(tm // CONV_ROWS):
        for j in range(D_CONV // LANES):
            lanes = slice(j * LANES, (j + 1) * LANES)
            ua_ref[c * CONV_ROWS:(c + 1) * CONV_ROWS, lanes] = (
                _causal_conv_tile(ext_ref, cwa_ref, lanes, c * CONV_ROWS, CONV_ROWS, A_TAIL, CONV_A_WIDTH)
                + cba_ref[:, lanes])
    tail = ext_ref[tm:tm + A_TAIL, :]
    tail_ref[0] = tail
    ext_ref[0:A_TAIL, :] = tail
    ya_ref[...] = _layer_norm_silu(ua_ref[...], lng_ref[...], lnb_ref[...]).astype(ya_ref.dtype)


def _inproj(x, norm_g, w_main, w_dt):
    m = x.shape[0]
    tm = min(m, ROW_TILE)
    row = lambda i: (i, 0)
    widths = (D_CONV, D_SSM, D_XBC, DT_PAD)
    dtypes = (jnp.float32, ACT_DTYPE, ACT_DTYPE, jnp.float32)
    return pl.pallas_call(
        _inproj_kernel,
        grid=(m // tm,),
        in_specs=[pl.BlockSpec((tm, D_MODEL), row), _const_spec(norm_g.shape), _const_spec(w_main.shape),
                  _const_spec(w_dt.shape)],
        out_specs=[pl.BlockSpec((tm, w), row) for w in widths],
        out_shape=[jax.ShapeDtypeStruct((m, w), d) for w, d in zip(widths, dtypes)],
        compiler_params=_params("arbitrary"),
        name="inproj",
    )(x, norm_g, w_main, w_dt)


def _inproj_conv(x, norm_g, w_main, w_dt, conv, seq_len):
    m = x.shape[0]
    tm = min(seq_len, ROW_TILE)
    tiles_per_seq = seq_len // tm
    f32 = jnp.float32
    row = lambda i: (i, 0)
    return pl.pallas_call(
        functools.partial(_inproj_conv_kernel, tiles_per_seq=tiles_per_seq),
        grid=(m // tm,),
        in_specs=[pl.BlockSpec((tm, D_MODEL), row), _const_spec(norm_g.shape), _const_spec(w_main.shape),
                  _const_spec(w_dt.shape)] + [_const_spec(a.shape) for a in conv],
        out_specs=[pl.BlockSpec((tm, D_CONV), row),
                   pl.BlockSpec((1, A_TAIL, D_CONV), lambda i: (i // tiles_per_seq, 0, 0))]
                  + [pl.BlockSpec((tm, w), row) for w in (D_SSM, D_XBC, DT_PAD)],
        out_shape=[jax.ShapeDtypeStruct((m, D_CONV), ACT_DTYPE),
                   jax.ShapeDtypeStruct((m // seq_len, A_TAIL, D_CONV), f32),
                   jax.ShapeDtypeStruct((m, D_SSM), ACT_DTYPE), jax.ShapeDtypeStruct((m, D_XBC), ACT_DTYPE),
                   jax.ShapeDtypeStruct((m, DT_PAD), f32)],
        scratch_shapes=[pltpu.VMEM((A_TAIL + tm, D_CONV), f32), pltpu.VMEM((tm, D_CONV), f32)],
        compiler_params=_params("arbitrary"),
        name="inproj_conv",
    )(x, norm_g, w_main, w_dt, *conv)


def _ssd_kernel(z_ref, xbc_ref, dt_ref,
                cwb_ref, cbb_ref, dtb_ref, a_ref, dexp_ref, nrm_ref, e_ref,
                ys_ref, ssm_ref,
                extb_ref, ht_ref, xc_ref, y_ref):
    c = pl.program_id(1)
    f32 = jnp.float32
    n_rows = SSD_CHUNKS_PER_STEP * CHUNK

    @pl.when(c == 0)
    def _():
        extb_ref[0:B_TAIL, :] = jnp.zeros((B_TAIL, D_XBC), f32)
        ht_ref[...] = jnp.zeros_like(ht_ref)

    extb_ref[B_TAIL:B_TAIL + n_rows, :] = xbc_ref[0].astype(f32)
    for q in range(SSD_CHUNKS_PER_STEP):
        for j in range(D_XBC // LANES):
            lanes = slice(j * LANES, (j + 1) * LANES)
            xc_ref[q * CHUNK:(q + 1) * CHUNK, lanes] = _silu(
                _causal_conv_tile(extb_ref, cwb_ref, lanes, q * CHUNK, CHUNK, B_TAIL, SSM_CONV_WIDTH)
                + cbb_ref[:, lanes])
    extb_ref[0:B_TAIL, :] = extb_ref[n_rows:n_rows + B_TAIL, :]

    rows = lax.broadcasted_iota(jnp.int32, (CHUNK, CHUNK), 0)
    cols = lax.broadcasted_iota(jnp.int32, (CHUNK, CHUNK), 1)
    causal = rows >= cols
    tri = causal.astype(MXU_DTYPE)
    lane = lax.broadcasted_iota(jnp.int32, (CHUNK, LANES), 1)
    e_mat = e_ref[...]
    gw = GROUP_WIDTH
    for q in range(SSD_CHUNKS_PER_STEP):
        rs = slice(q * CHUNK, (q + 1) * CHUNK)
        dt = _softplus(dt_ref[0, rs, :] + dtb_ref[...])
        a = dt * a_ref[...]
        a_cs = _dot_split_rhs(tri, a)
        a_cs_t = a_cs.T
        dt_t = dt.T
        ea = jnp.exp(a_cs)
        w_end = jnp.exp(a_cs[CHUNK - 1:CHUNK, :] - a_cs) * dt
        w_exp = _dot_split_lhs(w_end, e_mat)
        ea_exp = _dot_split_lhs(ea, e_mat)
        xs = xc_ref[rs, 0:D_SSM]
        xs_b = xs.astype(MXU_DTYPE)
        xw_b = (xs * w_exp).astype(MXU_DTYPE)
        for g in range(SSM_GROUPS):
            b_g = xc_ref[rs, D_SSM + g * D_STATE:D_SSM + (g + 1) * D_STATE]
            c_g = xc_ref[rs, D_SSM + D_BC + g * D_STATE:D_SSM + D_BC + (g + 1) * D_STATE]
            c_gb = c_g.astype(MXU_DTYPE)
            cb = _dot_nt(c_gb, b_g)
            ht_g = ht_ref[:, g * gw:(g + 1) * gw]
            y_off = _dot(c_gb, ht_g) * ea_exp[:, g * gw:(g + 1) * gw]
            for hp in range(HEADS_PER_GROUP // 2):
                res = []
                for e in range(2):
                    h = g * HEADS_PER_GROUP + 2 * hp + e
                    seg = a_cs[:, h:h + 1] - a_cs_t[h:h + 1, :]
                    l_mat = jnp.exp(jnp.where(causal, seg, NEG_BIG))
                    gmat = (cb * l_mat * dt_t[h:h + 1, :]).astype(MXU_DTYPE)
                    lo = g * gw + hp * LANES
                    res.append(jnp.dot(gmat, xs_b[:, lo:lo + LANES], preferred_element_type=f32))
                y_pair = jnp.where(lane < SSM_HEAD_DIM, res[0], res[1])
                y_ref[rs, lo:lo + LANES] = y_pair + y_off[:, hp * LANES:(hp + 1) * LANES]
            s_loc = _dot(b_g.T, xw_b[:, g * gw:(g + 1) * gw])
            ht_ref[:, g * gw:(g + 1) * gw] = ht_g * ea_exp[CHUNK - 1:CHUNK, g * gw:(g + 1) * gw] + s_loc

        y = y_ref[rs, :] + dexp_ref[...] * xs
        ys_ref[0, rs, :] = _gate_group_norm(y, z_ref[0, rs, :].astype(f32), nrm_ref[...]).astype(ys_ref.dtype)

    @pl.when(c == pl.num_programs(1) - 1)
    def _():
        for j in range(D_SSM // LANES):
            ssm_ref[0, j * LANES:(j + 1) * LANES, :] = ht_ref[:, j * LANES:(j + 1) * LANES].T


def _ssd(z, xbc, dt_raw, p):
    b, l, _ = z.shape
    n_rows = SSD_CHUNKS_PER_STEP * CHUNK
    tok = lambda i, c: (i, c, 0)
    consts = [p["cwb"], p["cbb"], p["dtb"], p["a"], p["dexp"], p["nrm"], p["e"]]
    f32 = jnp.float32
    return pl.pallas_call(
        _ssd_kernel,
        grid=(b, l // n_rows),
        in_specs=[pl.BlockSpec((1, n_rows, D_SSM), tok), pl.BlockSpec((1, n_rows, D_XBC), tok),
                  pl.BlockSpec((1, n_rows, DT_PAD), tok)] + [_const_spec(a.shape) for a in consts],
        out_specs=[pl.BlockSpec((1, n_rows, D_SSM), tok),
                   pl.BlockSpec((1, D_SSM, D_STATE), lambda i, c: (i, 0, 0))],
        out_shape=[jax.ShapeDtypeStruct((b, l, D_SSM), ACT_DTYPE),
                   jax.ShapeDtypeStruct((b, D_SSM, D_STATE), f32)],
        scratch_shapes=[pltpu.VMEM((B_TAIL + n_rows, D_XBC), f32),
                        pltpu.VMEM((D_STATE, D_SSM), f32),
                        pltpu.VMEM((n_rows, D_XBC), f32),
                        pltpu.VMEM((n_rows, D_SSM), f32)],
        compiler_params=_params("arbitrary", "arbitrary"),
        name="ssd",
    )(z, xbc, dt_raw, *consts)


def _sample_conv_a_kernel(st_ref, u_ref, w_ref, cba_ref, new_ref, ua_ref):
    hist = CONV_A_WIDTH - 1
    for j in range(u_ref.shape[1] // LANES):
        lanes = slice(j * LANES, (j + 1) * LANES)
        u = u_ref[:, lanes]
        acc = u * w_ref[hist:hist + 1, lanes] + cba_ref[:, lanes]
        for k in range(hist):
            row = st_ref[k, :, lanes]
            acc = acc + row * w_ref[k:k + 1, lanes]
            if k >= 1:
                new_ref[k - 1, :, lanes] = row
        new_ref[hist - 1, :, lanes] = u
        ua_ref[:, lanes] = acc


def _sample_conv_a(state_k, u, p):
    hist, s, _ = state_k.shape
    wb = SAMPLE_CONV_LANES
    f32 = jnp.float32
    blk3 = pl.BlockSpec((hist, s, wb), lambda j: (0, 0, j))
    blk2 = pl.BlockSpec((s, wb), lambda j: (0, j))
    return pl.pallas_call(
        _sample_conv_a_kernel,
        grid=(D_CONV // wb,),
        in_specs=[blk3, blk2, pl.BlockSpec((A_TAIL, wb), lambda j: (0, j)), pl.BlockSpec((1, wb), lambda j: (0, j))],
        out_specs=[blk3, blk2],
        out_shape=[jax.ShapeDtypeStruct((hist, s, D_CONV), f32), jax.ShapeDtypeStruct((s, D_CONV), f32)],
        compiler_params=_params("arbitrary"),
        name="sample_conv_a",
    )(state_k, u, p["cwa"], p["cba"])


def _sample_conv_b_kernel(st_ref, xbc_ref, dt_ref, cwb_ref, cbb_ref, dtb_ref, a_ref, e_ref,
                          new_ref, xs_ref, xt_ref, dat_ref, b_ref, ct_ref):
    hist = SSM_CONV_WIDTH - 1
    xbc = xbc_ref[...].astype(jnp.float32)
    acc = xbc * cwb_ref[hist:hist + 1, :] + cbb_ref[...]
    for k in range(hist):
        acc = acc + st_ref[k] * cwb_ref[k:k + 1, :]
    for k in range(hist - 1):
        new_ref[k] = st_ref[k + 1]
    new_ref[hist - 1] = xbc
    xc = _silu(acc)
    xs = xc[:, 0:D_SSM]
    dt = _softplus(dt_ref[...] + dtb_ref[...])
    da = jnp.exp(dt * a_ref[...])
    xs_ref[...] = xs
    x_dt = xs * _dot_split_lhs(dt, e_ref[...])
    for j in range(D_SSM // LANES):
        xt_ref[j * LANES:(j + 1) * LANES, :] = x_dt[:, j * LANES:(j + 1) * LANES].T.astype(xt_ref.dtype)
    dat_ref[...] = da.T[0:SSM_HEADS, :]
    b_ref[...] = xc[:, D_SSM:D_SSM + D_BC]
    for g in range(SSM_GROUPS):
        lo = D_SSM + D_BC + g * D_STATE
        ct_ref[g * D_STATE:(g + 1) * D_STATE, :] = xc[:, lo:lo + D_STATE].T.astype(ct_ref.dtype)


def _sample_conv_b(state_k, xbc, dt_raw, p):
    hist, s, _ = state_k.shape
    f32 = jnp.float32
    args = [state_k, xbc, dt_raw, p["cwb"], p["cbb"], p["dtb"], p["a"], p["e"]]
    shapes = [((hist, s, D_XBC), f32), ((s, D_SSM), f32), ((D_SSM, s), MXU_DTYPE), ((SSM_HEADS, s), f32),
              ((s, D_BC), f32), ((D_BC, s), MXU_DTYPE)]
    return pl.pallas_call(
        _sample_conv_b_kernel,
        grid=(1,),
        in_specs=[_const_spec(a.shape) for a in args],
        out_specs=[_full_spec(sh) for sh, _ in shapes],
        out_shape=[jax.ShapeDtypeStruct(sh, d) for sh, d in shapes],
        compiler_params=_params("arbitrary"),
        name="sample_conv_b",
    )(*args)


def _sample_ssd_kernel(dat_ref, st_ref, xt_ref, b_ref, ct_ref, new_ref, yt_ref):
    i = pl.program_id(0)
    bs = st_ref.shape[0]
    n_seq = yt_ref.shape[1]
    f32 = jnp.float32

    @pl.when(i == 0)
    def _():
        yt_ref[...] = jnp.zeros_like(yt_ref)

    row_id = lax.broadcasted_iota(jnp.int32, (n_seq, D_STATE), 0)
    col_id = lax.broadcasted_iota(jnp.int32, (GROUP_WIDTH, n_seq), 1)

    def body(sl, carry):
        s = i * bs + sl
        for g in range(SSM_GROUPS):
            rows = slice(g * GROUP_WIDTH, (g + 1) * GROUP_WIDTH)
            b_sel = jnp.where(row_id == s, b_ref[:, g * D_STATE:(g + 1) * D_STATE], 0.0).astype(MXU_DTYPE)
            outer = jnp.dot(xt_ref[rows, :], b_sel, preferred_element_type=f32)
            h_new = []
            for hh in range(HEADS_PER_GROUP):
                h = g * HEADS_PER_GROUP + hh
                hn = st_ref[sl, h] * dat_ref[h, s] + outer[hh * SSM_HEAD_DIM:(hh + 1) * SSM_HEAD_DIM, :]
                new_ref[sl, h] = hn
                h_new.append(hn.astype(MXU_DTYPE))
            y_all = jnp.dot(jnp.concatenate(h_new, axis=0), ct_ref[g * D_STATE:(g + 1) * D_STATE, :],
                            preferred_element_type=f32)
            yt_ref[rows, :] = jnp.where(col_id == s, y_all, yt_ref[rows, :])
        return carry

    lax.fori_loop(0, bs, body, 0, unroll=2)


def _sample_ssd(state, xt, dat, b, ct):
    s = state.shape[0]
    bs = SAMPLE_SSD_BLOCK
    blk = pl.BlockSpec((bs, SSM_HEADS, SSM_HEAD_DIM, D_STATE), lambda i: (i, 0, 0, 0))
    return pl.pallas_call(
        _sample_ssd_kernel,
        grid=(s // bs,),
        in_specs=[pl.BlockSpec(memory_space=pltpu.SMEM), blk,
                  _const_spec(xt.shape), _const_spec(b.shape), _const_spec(ct.shape)],
        out_specs=[blk, _full_spec((D_SSM, s))],
        out_shape=[jax.ShapeDtypeStruct(state.shape, jnp.float32), jax.ShapeDtypeStruct((D_SSM, s), jnp.float32)],
        compiler_params=_params("arbitrary"),
        name="sample_ssd",
    )(dat, state, xt, b, ct)


def _sample_gate_kernel(ua_ref, lng_ref, lnb_ref, yt_ref, xs_ref, z_ref, dexp_ref, nrm_ref, ya_ref, ys_ref):
    ya_ref[...] = _layer_norm_silu(ua_ref[...], lng_ref[...], lnb_ref[...]).astype(ya_ref.dtype)
    y = jnp.concatenate([yt_ref[j * LANES:(j + 1) * LANES, :].T for j in range(D_SSM // LANES)], axis=-1)
    y = y + dexp_ref[...] * xs_ref[...]
    ys_ref[...] = _gate_group_norm(y, z_ref[...].astype(jnp.float32), nrm_ref[...]).astype(ys_ref.dtype)


def _sample_gate(ua, yt, xs, z, p):
    s = xs.shape[0]
    args = [ua, p["lng"], p["lnb"], yt, xs, z, p["dexp"], p["nrm"]]
    return pl.pallas_call(
        _sample_gate_kernel,
        grid=(1,),
        in_specs=[_const_spec(a.shape) for a in args],
        out_specs=[_full_spec((s, D_CONV)), _full_spec((s, D_SSM))],
        out_shape=[jax.ShapeDtypeStruct((s, D_CONV), ACT_DTYPE), jax.ShapeDtypeStruct((s, D_SSM), ACT_DTYPE)],
        compiler_params=_params("arbitrary"),
        name="sample_gate",
    )(*args)


def _prep_params(i, conv_dw_w, conv_dw_b, conv_ln_g, conv_ln_b, ssm_conv_w, ssm_conv_b, ssm_dt_bias, ssm_a_log,
                 ssm_d, ssm_norm):
    f32 = jnp.float32
    row = lambda v: v.reshape(1, -1).astype(f32)
    pad_heads = lambda v: jnp.pad(v.astype(f32), (0, DT_PAD - SSM_HEADS)).reshape(1, DT_PAD)
    head_of_channel = jnp.arange(D_SSM, dtype=jnp.int32) // SSM_HEAD_DIM
    expand = (jnp.arange(DT_PAD, dtype=jnp.int32)[:, None] == head_of_channel[None, :]).astype(MXU_DTYPE)
    return {
        "cwa": jnp.pad(conv_dw_w[i].astype(f32), ((0, A_TAIL - CONV_A_WIDTH), (0, 0))),
        "cba": row(conv_dw_b[i]), "lng": row(conv_ln_g[i]), "lnb": row(conv_ln_b[i]),
        "cwb": jnp.pad(ssm_conv_w[i].astype(f32), ((0, B_TAIL - SSM_CONV_WIDTH), (0, 0))),
        "cbb": row(ssm_conv_b[i]),
        "dtb": pad_heads(ssm_dt_bias[i]),
        "a": pad_heads(-jnp.exp(ssm_a_log[i].astype(f32))),
        "dexp": jnp.repeat(ssm_d[i].astype(f32), SSM_HEAD_DIM).reshape(1, D_SSM),
        "nrm": row(ssm_norm[i]),
        "e": expand,
    }


def kernel(x_prompt, x_sample, state_conv_a, state_conv_b, state_ssm, ffn1_norm, ffn1_w_gate, ffn1_w_up, ffn1_w_down, mix_norm, w_in, conv_dw_w, conv_dw_b, conv_ln_g, conv_ln_b, ssm_conv_w, ssm_conv_b, ssm_dt_bias, ssm_a_log, ssm_d, ssm_norm, w_out, ffn2_norm, ffn2_w_gate, ffn2_w_up, ffn2_w_down, final_norm):
    depth = ffn1_norm.shape[0]
    bp, lp, _ = x_prompt.shape
    bs = x_sample.shape[0]
    f32 = jnp.float32
    row = lambda v: v.reshape(1, -1).astype(f32)
    w16 = lambda w: w.astype(MXU_DTYPE)

    yp = x_prompt.reshape(bp * lp, D_MODEL)
    ysm = x_sample.reshape(bs, D_MODEL)
    outs = [[] for _ in range(6)]
    for i in range(depth):
        p = _prep_params(i, conv_dw_w, conv_dw_b, conv_ln_g, conv_ln_b, ssm_conv_w, ssm_conv_b, ssm_dt_bias,
                         ssm_a_log, ssm_d, ssm_norm)
        n_main = 2 * D_CONV + D_SSM + D_XBC
        w_main = w16(w_in[i][:, :n_main])
        w_dt = w16(jnp.pad(w_in[i][:, n_main:], ((0, 0), (0, DT_PAD - SSM_HEADS))))
        f1 = (row(ffn1_norm[i]), w16(ffn1_w_gate[i]), w16(ffn1_w_up[i]), w16(ffn1_w_down[i]))
        f2 = (row(ffn2_norm[i]), w16(ffn2_w_gate[i]), w16(ffn2_w_up[i]), w16(ffn2_w_down[i]))
        wo = w16(w_out[i])
        fin = row(final_norm) if i == depth - 1 else None

        x1 = _ffn(yp, *f1)
        ya, a_tail, z, xbc, dt_raw = _inproj_conv(x1, row(mix_norm[i]), w_main, w_dt,
                                                  (p["cwa"], p["cba"], p["lng"], p["lnb"]), lp)
        xbc3 = xbc.reshape(bp, lp, D_XBC)
        ys, ssm_t = _ssd(z.reshape(bp, lp, D_SSM), xbc3, dt_raw.reshape(bp, lp, DT_PAD), p)
        yp = _ffn(x1, *f2, ya=ya, ys=ys.reshape(bp * lp, D_SSM), wo=wo, final_g=fin)
        outs[0].append(a_tail[:, A_TAIL - (CONV_A_WIDTH - 1):, :].astype(x_prompt.dtype))
        outs[1].append(xbc3[:, lp - (SSM_CONV_WIDTH - 1):, :].astype(x_prompt.dtype))
        outs[2].append(ssm_t.reshape(bp, SSM_HEADS, SSM_HEAD_DIM, D_STATE).astype(state_ssm.dtype))

        s1 = _ffn(ysm, *f1)
        su, sz, sxbc, sdt = _inproj(s1, row(mix_norm[i]), w_main, w_dt)
        tap_major = lambda st: jnp.transpose(st.astype(f32), (1, 0, 2))
        new_a, sua = _sample_conv_a(tap_major(state_conv_a[i]), su, p)
        new_b, xs, xt, dat, b_mat, ct = _sample_conv_b(tap_major(state_conv_b[i]), sxbc, sdt, p)
        new_s, yt = _sample_ssd(state_ssm[i].astype(f32), xt, dat, b_mat, ct)
        sya, sys_ = _sample_gate(sua, yt, xs, sz, p)
        ysm = _ffn(s1, *f2, ya=sya, ys=sys_, wo=wo, final_g=fin)
        outs[3].append(jnp.transpose(new_a, (1, 0, 2)).astype(x_sample.dtype))
        outs[4].append(jnp.transpose(new_b, (1, 0, 2)).astype(x_sample.dtype))
        outs[5].append(new_s.astype(state_ssm.dtype))

    return (yp.reshape(bp, lp, D_MODEL), ysm.reshape(bs, 1, D_MODEL)) + tuple(jnp.stack(o) for o in outs)
```

```python
import functools

import jax
import jax.numpy as jnp
from jax import lax
from jax.experimental import pallas as pl
from jax.experimental.pallas import tpu as pltpu

D_MODEL = 1024
D_FF = 2816
D_CONV = 1024
CONV_A_WIDTH = 31
D_SSM = 1024
SSM_HEAD_DIM = 64
SSM_HEADS = 16
SSM_GROUPS = 2
HEADS_PER_GROUP = SSM_HEADS // SSM_GROUPS
GROUP_WIDTH = HEADS_PER_GROUP * SSM_HEAD_DIM
D_STATE = 128
SSM_CONV_WIDTH = 4
CHUNK = 128
D_XBC = D_SSM + 2 * SSM_GROUPS * D_STATE
D_BC = SSM_GROUPS * D_STATE
FFN_RES_WEIGHT = 0.5
NORM_EPS = 1e-5

LANES = 128
SUBLANES = 8
DT_PAD = LANES
A_TAIL = 32
B_TAIL = 8
ROW_TILE = 512
CONV_ROWS = 128
SSD_CHUNKS_PER_STEP = 8
SAMPLE_SSD_BLOCK = 8
SAMPLE_CONV_LANES = 256
VMEM_LIMIT = 56 * 1024 * 1024
NEG_BIG = -1e30

ACT_DTYPE = jnp.bfloat16
MXU_DTYPE = jnp.bfloat16


def _dot(a, b):
    return jnp.dot(a.astype(MXU_DTYPE), b.astype(MXU_DTYPE), preferred_element_type=jnp.float32)


def _dot_nt(a, b):
    return lax.dot_general(a.astype(MXU_DTYPE), b.astype(MXU_DTYPE), (((1,), (1,)), ((), ())),
                           preferred_element_type=jnp.float32)


def _split2(v):
    hi = v.astype(MXU_DTYPE)
    lo = (v - hi.astype(jnp.float32)).astype(MXU_DTYPE)
    return hi, lo


def _dot_split_lhs(v, m):
    hi, lo = _split2(v)
    return (jnp.dot(hi, m, preferred_element_type=jnp.float32)
            + jnp.dot(lo, m, preferred_element_type=jnp.float32))


def _dot_split_rhs(m, v):
    hi = v.astype(MXU_DTYPE)
    r1 = v - hi.astype(jnp.float32)
    mid = r1.astype(MXU_DTYPE)
    lo = (r1 - mid.astype(jnp.float32)).astype(MXU_DTYPE)
    return (jnp.dot(m, hi, preferred_element_type=jnp.float32)
            + jnp.dot(m, mid, preferred_element_type=jnp.float32)
            + jnp.dot(m, lo, preferred_element_type=jnp.float32))


def _rms_norm(x, g):
    return x * lax.rsqrt(jnp.mean(x * x, axis=-1, keepdims=True) + NORM_EPS) * g


def _silu(x):
    return x * jax.nn.sigmoid(x)


def _softplus(x):
    return jnp.maximum(x, 0.0) + jnp.log1p(jnp.exp(-jnp.abs(x)))


def _layer_norm_silu(x, g, b):
    mu = jnp.mean(x, axis=-1, keepdims=True)
    xc = x - mu
    var = jnp.mean(xc * xc, axis=-1, keepdims=True)
    return _silu(xc * lax.rsqrt(var + NORM_EPS) * g + b)


def _gate_group_norm(y, z, g):
    y = y * _silu(z)
    parts = []
    for i in range(SSM_GROUPS):
        yg = y[:, i * GROUP_WIDTH:(i + 1) * GROUP_WIDTH]
        parts.append(yg * lax.rsqrt(jnp.mean(yg * yg, axis=-1, keepdims=True) + NORM_EPS))
    return jnp.concatenate(parts, axis=-1) * g


def _causal_conv_tile(ext_ref, w_ref, lanes, row0, n_out, tail, width):
    n_rows = tail + n_out
    x = ext_ref[row0:row0 + n_rows, lanes]
    first = tail - (width - 1)
    acc = jnp.zeros((n_out, LANES), jnp.float32)
    for r in range(SUBLANES):
        taps = [k for k in range(width) if (first + k) % SUBLANES == r]
        if not taps:
            continue
        xr = x if r == 0 else pltpu.roll(x, n_rows - r, 0)
        for k in taps:
            base = first + k - r
            acc = acc + xr[base:base + n_out, :] * w_ref[k:k + 1, lanes]
    return acc


def _const_spec(shape):
    nd = len(shape)
    return pl.BlockSpec(shape, lambda *_: (0,) * nd, pipeline_mode=pl.Buffered(1))


def _full_spec(shape):
    nd = len(shape)
    return pl.BlockSpec(shape, lambda *_: (0,) * nd)


def _params(*semantics):
    return pltpu.CompilerParams(dimension_semantics=semantics, vmem_limit_bytes=VMEM_LIMIT)


def _ffn_kernel(*refs, with_outproj, with_final_norm, n_cast):
    refs = list(refs)
    x_ref = refs.pop(0)
    if with_outproj:
        ya_ref, ys_ref, wo_ref = refs[:3]
        refs = refs[3:]
    g_ref, wg_ref, wu_ref, wd_ref = refs[:4]
    refs = refs[4:]
    if with_final_norm:
        fg_ref = refs.pop(0)
    cast_in = refs[:n_cast]
    o_ref = refs[n_cast]
    cast_out = refs[n_cast + 1:]
    for src, dst in zip(cast_in, cast_out):
        dst[...] = src[...].astype(dst.dtype)

    x = x_ref[...]
    if with_outproj:
        x = x + _dot(ya_ref[...], wo_ref[0:D_CONV, :]) + _dot(ys_ref[...], wo_ref[D_CONV:D_CONV + D_SSM, :])
    xn = _rms_norm(x, g_ref[...]).astype(MXU_DTYPE)
    gate = jnp.dot(xn, wg_ref[...], preferred_element_type=jnp.float32)
    up = jnp.dot(xn, wu_ref[...], preferred_element_type=jnp.float32)
    h = (_silu(gate) * up).astype(MXU_DTYPE)
    y = x + FFN_RES_WEIGHT * jnp.dot(h, wd_ref[...], preferred_element_type=jnp.float32)
    if with_final_norm:
        y = _rms_norm(y, fg_ref[...])
    o_ref[...] = y


def _cast_block_rows(n_rows, n_steps):
    tile = 2 * SUBLANES
    n_blocks = max(d for d in range(1, n_steps + 1) if n_rows % d == 0 and (n_rows // d) % tile == 0)
    return n_rows // n_blocks


def _ffn(x, norm_g, wg, wu, wd, ya=None, ys=None, wo=None, final_g=None, cast=()):
    m = x.shape[0]
    tm = min(m, ROW_TILE)
    n_steps = m // tm
    with_outproj = ya is not None
    with_final_norm = final_g is not None
    row = lambda i: (i, 0)
    args, specs = [x], [pl.BlockSpec((tm, D_MODEL), row)]
    if with_outproj:
        args += [ya, ys, wo]
        specs += [pl.BlockSpec((tm, D_CONV), row), pl.BlockSpec((tm, D_SSM), row), _const_spec(wo.shape)]
    args += [norm_g, wg, wu, wd]
    specs += [_const_spec(norm_g.shape), _const_spec(wg.shape), _const_spec(wu.shape), _const_spec(wd.shape)]
    if with_final_norm:
        args.append(final_g)
        specs.append(_const_spec(final_g.shape))
    cast_specs = []
    for w in cast:
        br = _cast_block_rows(w.shape[0], n_steps)
        last = w.shape[0] // br - 1
        cast_specs.append(pl.BlockSpec((br, w.shape[1]), lambda i, last=last: (jnp.minimum(i, last), 0)))
    out = pl.pallas_call(
        functools.partial(_ffn_kernel, with_outproj=with_outproj, with_final_norm=with_final_norm,
                          n_cast=len(cast)),
        grid=(n_steps,),
        in_specs=specs + cast_specs,
        out_specs=[pl.BlockSpec((tm, D_MODEL), row)] + cast_specs,
        out_shape=[jax.ShapeDtypeStruct((m, D_MODEL), jnp.float32)]
                  + [jax.ShapeDtypeStruct(w.shape, MXU_DTYPE) for w in cast],
        compiler_params=_params("arbitrary"),
        name="ffn_out" if with_outproj else "ffn_in",
    )(*args, *cast)
    return out if cast else out[0]


def _project(x_ref, g_ref, w_ref, wdt_ref, z_ref, xbc_ref, dt_ref):
    xn = _rms_norm(x_ref[...], g_ref[...]).astype(MXU_DTYPE)

    def proj(lo, width):
        return jnp.dot(xn, w_ref[:, lo:lo + width], preferred_element_type=jnp.float32)

    u = proj(0, D_CONV) * jax.nn.sigmoid(proj(D_CONV, D_CONV))
    z_ref[...] = proj(2 * D_CONV, D_SSM).astype(z_ref.dtype)
    xbc_ref[...] = proj(2 * D_CONV + D_SSM, D_XBC).astype(xbc_ref.dtype)
    dt_ref[...] = jnp.dot(xn, wdt_ref[...], preferred_element_type=jnp.float32)
    return u


def _inproj_kernel(x_ref, g_ref, w_ref, wdt_ref, u_ref, z_ref, xbc_ref, dt_ref):
    u_ref[...] = _project(x_ref, g_ref, w_ref, wdt_ref, z_ref, xbc_ref, dt_ref)


def _inproj_conv_kernel(x_ref, g_ref, w_ref, wdt_ref, cwa_ref, cba_ref, lng_ref, lnb_ref,
                        ya_ref, tail_ref, z_ref, xbc_ref, dt_ref, ext_ref, ua_ref, *, tiles_per_seq):
    tm = x_ref.shape[0]

    @pl.when(pl.program_id(0) % tiles_per_seq == 0)
    def _():
        ext_ref[0:A_TAIL, :] = jnp.zeros((A_TAIL, D_CONV), jnp.float32)

    ext_ref[A_TAIL:A_TAIL + tm, :] = _project(x_ref, g_ref, w_ref, wdt_ref, z_ref, xbc_ref, dt_ref)
    for c in range(tm // CONV_ROWS):
        for j in range(D_CONV // LANES):
            lanes = slice(j * LANES, (j + 1) * LANES)
            ua_ref[c * CONV_ROWS:(c + 1) * CONV_ROWS, lanes] = (
                _causal_conv_tile(ext_ref, cwa_ref, lanes, c * CONV_ROWS, CONV_ROWS, A_TAIL, CONV_A_WIDTH)
                + cba_ref[:, lanes])
    tail = ext_ref[tm:tm + A_TAIL, :]
    tail_ref[0] = tail
    ext_ref[0:A_TAIL, :] = tail
    ya_ref[...] = _layer_norm_silu(ua_ref[...], lng_ref[...], lnb_ref[...]).astype(ya_ref.dtype)


def _inproj(x, norm_g, w_main, w_dt):
    m = x.shape[0]
    tm = min(m, ROW_TILE)
    row = lambda i: (i, 0)
    widths = (D_CONV, D_SSM, D_XBC, DT_PAD)
    dtypes = (jnp.float32, ACT_DTYPE, ACT_DTYPE, jnp.float32)
    return pl.pallas_call(
        _inproj_kernel,
        grid=(m // tm,),
        in_specs=[pl.BlockSpec((tm, D_MODEL), row), _const_spec(norm_g.shape), _const_spec(w_main.shape),
                  _const_spec(w_dt.shape)],
        out_specs=[pl.BlockSpec((tm, w), row) for w in widths],
        out_shape=[jax.ShapeDtypeStruct((m, w), d) for w, d in zip(widths, dtypes)],
        compiler_params=_params("arbitrary"),
        name="inproj",
    )(x, norm_g, w_main, w_dt)


def _inproj_conv(x, norm_g, w_main, w_dt, conv, seq_len):
    m = x.shape[0]
    tm = min(seq_len, ROW_TILE)
    tiles_per_seq = seq_len // tm
    f32 = jnp.float32
    row = lambda i: (i, 0)
    return pl.pallas_call(
        functools.partial(_inproj_conv_kernel, tiles_per_seq=tiles_per_seq),
        grid=(m // tm,),
        in_specs=[pl.BlockSpec((tm, D_MODEL), row), _const_spec(norm_g.shape), _const_spec(w_main.shape),
                  _const_spec(w_dt.shape)] + [_const_spec(a.shape) for a in conv],
        out_specs=[pl.BlockSpec((tm, D_CONV), row),
                   pl.BlockSpec((1, A_TAIL, D_CONV), lambda i: (i // tiles_per_seq, 0, 0))]
                  + [pl.BlockSpec((tm, w), row) for w in (D_SSM, D_XBC, DT_PAD)],
        out_shape=[jax.ShapeDtypeStruct((m, D_CONV), ACT_DTYPE),
                   jax.ShapeDtypeStruct((m // seq_len, A_TAIL, D_CONV), f32),
                   jax.ShapeDtypeStruct((m, D_SSM), ACT_DTYPE), jax.ShapeDtypeStruct((m, D_XBC), ACT_DTYPE),
                   jax.ShapeDtypeStruct((m, DT_PAD), f32)],
        scratch_shapes=[pltpu.VMEM((A_TAIL + tm, D_CONV), f32), pltpu.VMEM((tm, D_CONV), f32)],
        compiler_params=_params("arbitrary"),
        name="inproj_conv",
    )(x, norm_g, w_main, w_dt, *conv)


def _ssd_kernel(z_ref, xbc_ref, dt_ref,
                cwb_ref, cbb_ref, dtb_ref, a_ref, dexp_ref, nrm_ref, e_ref,
                ys_ref, ssm_ref,
                extb_ref, ht_ref, xc_ref, y_ref):
    c = pl.program_id(1)
    f32 = jnp.float32
    n_rows = SSD_CHUNKS_PER_STEP * CHUNK

    @pl.when(c == 0)
    def _():
        extb_ref[0:B_TAIL, :] = jnp.zeros((B_TAIL, D_XBC), f32)
        ht_ref[...] = jnp.zeros_like(ht_ref)

    extb_ref[B_TAIL:B_TAIL + n_rows, :] = xbc_ref[0].astype(f32)
    for q in range(SSD_CHUNKS_PER_STEP):
        for j in range(D_XBC // LANES):
            lanes = slice(j * LANES, (j + 1) * LANES)
            xc_ref[q * CHUNK:(q + 1) * CHUNK, lanes] = _silu(
                _causal_conv_tile(extb_ref, cwb_ref, lanes, q * CHUNK, CHUNK, B_TAIL, SSM_CONV_WIDTH)
                + cbb_ref[:, lanes])
    extb_ref[0:B_TAIL, :] = extb_ref[n_rows:n_rows + B_TAIL, :]

    rows = lax.broadcasted_iota(jnp.int32, (CHUNK, CHUNK), 0)
    cols = lax.broadcasted_iota(jnp.int32, (CHUNK, CHUNK), 1)
    causal = rows >= cols
    tri = causal.astype(MXU_DTYPE)
    lane = lax.broadcasted_iota(jnp.int32, (CHUNK, LANES), 1)
    e_mat = e_ref[...]
    gw = GROUP_WIDTH
    for q in range(SSD_CHUNKS_PER_STEP):
        rs = slice(q * CHUNK, (q + 1) * CHUNK)
        dt = _softplus(dt_ref[0, rs, :] + dtb_ref[...])
        a = dt * a_ref[...]
        a_cs = _dot_split_rhs(tri, a)
        a_cs_t = a_cs.T
        dt_t = dt.T
        ea = jnp.exp(a_cs)
        w_end = jnp.exp(a_cs[CHUNK - 1:CHUNK, :] - a_cs) * dt
        w_exp = _dot_split_lhs(w_end, e_mat)
        ea_exp = _dot_split_lhs(ea, e_mat)
        xs = xc_ref[rs, 0:D_SSM]
        xs_b = xs.astype(MXU_DTYPE)
        xw_b = (xs * w_exp).astype(MXU_DTYPE)
        for g in range(SSM_GROUPS):
            b_g = xc_ref[rs, D_SSM + g * D_STATE:D_SSM + (g + 1) * D_STATE]
            c_g = xc_ref[rs, D_SSM + D_BC + g * D_STATE:D_SSM + D_BC + (g + 1) * D_STATE]
            c_gb = c_g.astype(MXU_DTYPE)
            cb = _dot_nt(c_gb, b_g)
            ht_g = ht_ref[:, g * gw:(g + 1) * gw]
            y_off = _dot(c_gb, ht_g) * ea_exp[:, g * gw:(g + 1) * gw]
            for hp in range(HEADS_PER_GROUP // 2):
                res = []
                for e in range(2):
                    h = g * HEADS_PER_GROUP + 2 * hp + e
                    seg = a_cs[:, h:h + 1] - a_cs_t[h:h + 1, :]
                    l_mat = jnp.exp(jnp.where(causal, seg, NEG_BIG))
                    gmat = (cb * l_mat * dt_t[h:h + 1, :]).astype(MXU_DTYPE)
                    lo = g * gw + hp * LANES
                    res.append(jnp.dot(gmat, xs_b[:, lo:lo + LANES], preferred_element_type=f32))
                y_pair = jnp.where(lane < SSM_HEAD_DIM, res[0], res[1])
                y_ref[rs, lo:lo + LANES] = y_pair + y_off[:, hp * LANES:(hp + 1) * LANES]
            s_loc = _dot(b_g.T, xw_b[:, g * gw:(g + 1) * gw])
            ht_ref[:, g * gw:(g + 1) * gw] = ht_g * ea_exp[CHUNK - 1:CHUNK, g * gw:(g + 1) * gw] + s_loc

        y = y_ref[rs, :] + dexp_ref[...] * xs
        ys_ref[0, rs, :] = _gate_group_norm(y, z_ref[0, rs, :].astype(f32), nrm_ref[...]).astype(ys_ref.dtype)

    @pl.when(c == pl.num_programs(1) - 1)
    def _():
        for j in range(D_SSM // LANES):
            ssm_ref[0, j * LANES:(j + 1) * LANES, :] = ht_ref[:, j * LANES:(j + 1) * LANES].T


def _ssd(z, xbc, dt_raw, p):
    b, l, _ = z.shape
    n_rows = SSD_CHUNKS_PER_STEP * CHUNK
    tok = lambda i, c: (i, c, 0)
    consts = [p["cwb"], p["cbb"], p["dtb"], p["a"], p["dexp"], p["nrm"], p["e"]]
    f32 = jnp.float32
    return pl.pallas_call(
        _ssd_kernel,
        grid=(b, l // n_rows),
        in_specs=[pl.BlockSpec((1, n_rows, D_SSM), tok), pl.BlockSpec((1, n_rows, D_XBC), tok),
                  pl.BlockSpec((1, n_rows, DT_PAD), tok)] + [_const_spec(a.shape) for a in consts],
        out_specs=[pl.BlockSpec((1, n_rows, D_SSM), tok),
                   pl.BlockSpec((1, D_SSM, D_STATE), lambda i, c: (i, 0, 0))],
        out_shape=[jax.ShapeDtypeStruct((b, l, D_SSM), ACT_DTYPE),
                   jax.ShapeDtypeStruct((b, D_SSM, D_STATE), f32)],
        scratch_shapes=[pltpu.VMEM((B_TAIL + n_rows, D_XBC), f32),
                        pltpu.VMEM((D_STATE, D_SSM), f32),
                        pltpu.VMEM((n_rows, D_XBC), f32),
                        pltpu.VMEM((n_rows, D_SSM), f32)],
        compiler_params=_params("arbitrary", "arbitrary"),
        name="ssd",
    )(z, xbc, dt_raw, *consts)


def _sample_conv_a_kernel(st_ref, u_ref, w_ref, cba_ref, new_ref, ua_ref):
    hist = CONV_A_WIDTH - 1
    for j in range(u_ref.shape[1] // LANES):
        lanes = slice(j * LANES, (j + 1) * LANES)
        u = u_ref[:, lanes]
        acc = u * w_ref[hist:hist + 1, lanes] + cba_ref[:, lanes]
        for k in range(hist):
            row = st_ref[k, :, lanes]
            acc = acc + row * w_ref[k:k + 1, lanes]
            if k >= 1:
                new_ref[k - 1, :, lanes] = row
        new_ref[hist - 1, :, lanes] = u
        ua_ref[:, lanes] = acc


def _sample_conv_a(state_k, u, p):
    hist, s, _ = state_k.shape
    wb = SAMPLE_CONV_LANES
    f32 = jnp.float32
    blk3 = pl.BlockSpec((hist, s, wb), lambda j: (0, 0, j))
    blk2 = pl.BlockSpec((s, wb), lambda j: (0, j))
    return pl.pallas_call(
        _sample_conv_a_kernel,
        grid=(D_CONV // wb,),
        in_specs=[blk3, blk2, pl.BlockSpec((A_TAIL, wb), lambda j: (0, j)), pl.BlockSpec((1, wb), lambda j: (0, j))],
        out_specs=[blk3, blk2],
        out_shape=[jax.ShapeDtypeStruct((hist, s, D_CONV), f32), jax.ShapeDtypeStruct((s, D_CONV), f32)],
        compiler_params=_params("arbitrary"),
        name="sample_conv_a",
    )(state_k, u, p["cwa"], p["cba"])


def _sample_conv_b_kernel(st_ref, xbc_ref, dt_ref, cwb_ref, cbb_ref, dtb_ref, a_ref, e_ref,
                          new_ref, xs_ref, xt_ref, dat_ref, b_ref, ct_ref):
    hist = SSM_CONV_WIDTH - 1
    xbc = xbc_ref[...].astype(jnp.float32)
    acc = xbc * cwb_ref[hist:hist + 1, :] + cbb_ref[...]
    for k in range(hist):
        acc = acc + st_ref[k] * cwb_ref[k:k + 1, :]
    for k in range(hist - 1):
        new_ref[k] = st_ref[k + 1]
    new_ref[hist - 1] = xbc
    xc = _silu(acc)
    xs = xc[:, 0:D_SSM]
    dt = _softplus(dt_ref[...] + dtb_ref[...])
    da = jnp.exp(dt * a_ref[...])
    xs_ref[...] = xs
    x_dt = xs * _dot_split_lhs(dt, e_ref[...])
    for j in range(D_SSM // LANES):
        xt_ref[j * LANES:(j + 1) * LANES, :] = x_dt[:, j * LANES:(j + 1) * LANES].T.astype(xt_ref.dtype)
    dat_ref[...] = da.T[0:SSM_HEADS, :]
    b_ref[...] = xc[:, D_SSM:D_SSM + D_BC]
    for g in range(SSM_GROUPS):
        lo = D_SSM + D_BC + g * D_STATE
        ct_ref[g * D_STATE:(g + 1) * D_STATE, :] = xc[:, lo:lo + D_STATE].T.astype(ct_ref.dtype)


def _sample_conv_b(state_k, xbc, dt_raw, p):
    hist, s, _ = state_k.shape
    f32 = jnp.float32
    args = [state_k, xbc, dt_raw, p["cwb"], p["cbb"], p["dtb"], p["a"], p["e"]]
    shapes = [((hist, s, D_XBC), f32), ((s, D_SSM), f32), ((D_SSM, s), MXU_DTYPE), ((SSM_HEADS, s), f32),
              ((s, D_BC), f32), ((D_BC, s), MXU_DTYPE)]
    return pl.pallas_call(
        _sample_conv_b_kernel,
        grid=(1,),
        in_specs=[_const_spec(a.shape) for a in args],
        out_specs=[_full_spec(sh) for sh, _ in shapes],
        out_shape=[jax.ShapeDtypeStruct(sh, d) for sh, d in shapes],
        compiler_params=_params("arbitrary"),
        name="sample_conv_b",
    )(*args)


def _sample_ssd_kernel(dat_ref, st_ref, xt_ref, b_ref, ct_ref, new_ref, yt_ref):
    i = pl.program_id(0)
    bs = st_ref.shape[0]
    n_seq = yt_ref.shape[1]
    f32 = jnp.float32

    @pl.when(i == 0)
    def _():
        yt_ref[...] = jnp.zeros_like(yt_ref)

    row_id = lax.broadcasted_iota(jnp.int32, (n_seq, D_STATE), 0)
    col_id = lax.broadcasted_iota(jnp.int32, (GROUP_WIDTH, n_seq), 1)

    def body(sl, carry):
        s = i * bs + sl
        for g in range(SSM_GROUPS):
            rows = slice(g * GROUP_WIDTH, (g + 1) * GROUP_WIDTH)
            b_sel = jnp.where(row_id == s, b_ref[:, g * D_STATE:(g + 1) * D_STATE], 0.0).astype(MXU_DTYPE)
            outer = jnp.dot(xt_ref[rows, :], b_sel, preferred_element_type=f32)
            h_new = []
            for hh in range(HEADS_PER_GROUP):
                h = g * HEADS_PER_GROUP + hh
                hn = st_ref[sl, h] * dat_ref[h, s] + outer[hh * SSM_HEAD_DIM:(hh + 1) * SSM_HEAD_DIM, :]
                new_ref[sl, h] = hn
                h_new.append(hn.astype(MXU_DTYPE))
            y_all = jnp.dot(jnp.concatenate(h_new, axis=0), ct_ref[g * D_STATE:(g + 1) * D_STATE, :],
                            preferred_element_type=f32)
            yt_ref[rows, :] = jnp.where(col_id == s, y_all, yt_ref[rows, :])
        return carry

    lax.fori_loop(0, bs, body, 0, unroll=2)


def _sample_ssd(state, xt, dat, b, ct):
    s = state.shape[0]
    bs = SAMPLE_SSD_BLOCK
    blk = pl.BlockSpec((bs, SSM_HEADS, SSM_HEAD_DIM, D_STATE), lambda i: (i, 0, 0, 0))
    return pl.pallas_call(
        _sample_ssd_kernel,
        grid=(s // bs,),
        in_specs=[pl.BlockSpec(memory_space=pltpu.SMEM), blk,
                  _const_spec(xt.shape), _const_spec(b.shape), _const_spec(ct.shape)],
        out_specs=[blk, _full_spec((D_SSM, s))],
        out_shape=[jax.ShapeDtypeStruct(state.shape, jnp.float32), jax.ShapeDtypeStruct((D_SSM, s), jnp.float32)],
        compiler_params=_params("arbitrary"),
        name="sample_ssd",
    )(dat, state, xt, b, ct)


def _sample_gate_kernel(ua_ref, lng_ref, lnb_ref, yt_ref, xs_ref, z_ref, dexp_ref, nrm_ref, ya_ref, ys_ref):
    ya_ref[...] = _layer_norm_silu(ua_ref[...], lng_ref[...], lnb_ref[...]).astype(ya_ref.dtype)
    y = jnp.concatenate([yt_ref[j * LANES:(j + 1) * LANES, :].T for j in range(D_SSM // LANES)], axis=-1)
    y = y + dexp_ref[...] * xs_ref[...]
    ys_ref[...] = _gate_group_norm(y, z_ref[...].astype(jnp.float32), nrm_ref[...]).astype(ys_ref.dtype)


def _sample_gate(ua, yt, xs, z, p):
    s = xs.shape[0]
    args = [ua, p["lng"], p["lnb"], yt, xs, z, p["dexp"], p["nrm"]]
    return pl.pallas_call(
        _sample_gate_kernel,
        grid=(1,),
        in_specs=[_const_spec(a.shape) for a in args],
        out_specs=[_full_spec((s, D_CONV)), _full_spec((s, D_SSM))],
        out_shape=[jax.ShapeDtypeStruct((s, D_CONV), ACT_DTYPE), jax.ShapeDtypeStruct((s, D_SSM), ACT_DTYPE)],
        compiler_params=_params("arbitrary"),
        name="sample_gate",
    )(*args)


def _prep_params(i, conv_dw_w, conv_dw_b, conv_ln_g, conv_ln_b, ssm_conv_w, ssm_conv_b, ssm_dt_bias, ssm_a_log,
                 ssm_d, ssm_norm):
    f32 = jnp.float32
    row = lambda v: v.reshape(1, -1).astype(f32)
    pad_heads = lambda v: jnp.pad(v.astype(f32), (0, DT_PAD - SSM_HEADS)).reshape(1, DT_PAD)
    head_of_channel = jnp.arange(D_SSM, dtype=jnp.int32) // SSM_HEAD_DIM
    expand = (jnp.arange(DT_PAD, dtype=jnp.int32)[:, None] == head_of_channel[None, :]).astype(MXU_DTYPE)
    return {
        "cwa": jnp.pad(conv_dw_w[i].astype(f32), ((0, A_TAIL - CONV_A_WIDTH), (0, 0))),
        "cba": row(conv_dw_b[i]), "lng": row(conv_ln_g[i]), "lnb": row(conv_ln_b[i]),
        "cwb": jnp.pad(ssm_conv_w[i].astype(f32), ((0, B_TAIL - SSM_CONV_WIDTH), (0, 0))),
        "cbb": row(ssm_conv_b[i]),
        "dtb": pad_heads(ssm_dt_bias[i]),
        "a": pad_heads(-jnp.exp(ssm_a_log[i].astype(f32))),
        "dexp": jnp.repeat(ssm_d[i].astype(f32), SSM_HEAD_DIM).reshape(1, D_SSM),
        "nrm": row(ssm_norm[i]),
        "e": expand,
    }


def kernel(x_prompt, x_sample, state_conv_a, state_conv_b, state_ssm, ffn1_norm, ffn1_w_gate, ffn1_w_up, ffn1_w_down, mix_norm, w_in, conv_dw_w, conv_dw_b, conv_ln_g, conv_ln_b, ssm_conv_w, ssm_conv_b, ssm_dt_bias, ssm_a_log, ssm_d, ssm_norm, w_out, ffn2_norm, ffn2_w_gate, ffn2_w_up, ffn2_w_down, final_norm):
    depth = ffn1_norm.shape[0]
    bp, lp, _ = x_prompt.shape
    bs = x_sample.shape[0]
    f32 = jnp.float32
    row = lambda v: v.reshape(1, -1).astype(f32)
    w16 = lambda w: w.astype(MXU_DTYPE)

    yp = x_prompt.reshape(bp * lp, D_MODEL)
    ysm = x_sample.reshape(bs, D_MODEL)
    outs = [[] for _ in range(6)]
    for i in range(depth):
        p = _prep_params(i, conv_dw_w, conv_dw_b, conv_ln_g, conv_ln_b, ssm_conv_w, ssm_conv_b, ssm_dt_bias,
                         ssm_a_log, ssm_d, ssm_norm)
        n_main = 2 * D_CONV + D_SSM + D_XBC
        w_main = w16(w_in[i][:, :n_main])
        w_dt = w16(jnp.pad(w_in[i][:, n_main:], ((0, 0), (0, DT_PAD - SSM_HEADS))))
        f1 = (row(ffn1_norm[i]), w16(ffn1_w_gate[i]), w16(ffn1_w_up[i]), w16(ffn1_w_down[i]))
        fin = row(final_norm) if i == depth - 1 else None

        x1, w2g, w2u, w2d, wo = _ffn(yp, *f1, cast=(ffn2_w_gate[i], ffn2_w_up[i], ffn2_w_down[i], w_out[i]))
        f2 = (row(ffn2_norm[i]), w2g, w2u, w2d)
        ya, a_tail, z, xbc, dt_raw = _inproj_conv(x1, row(mix_norm[i]), w_main, w_dt,
                                                  (p["cwa"], p["cba"], p["lng"], p["lnb"]), lp)
        xbc3 = xbc.reshape(bp, lp, D_XBC)
        ys, ssm_t = _ssd(z.reshape(bp, lp, D_SSM), xbc3, dt_raw.reshape(bp, lp, DT_PAD), p)
        yp = _ffn(x1, *f2, ya=ya, ys=ys.reshape(bp * lp, D_SSM), wo=wo, final_g=fin)
        outs[0].append(a_tail[:, A_TAIL - (CONV_A_WIDTH - 1):, :].astype(x_prompt.dtype))
        outs[1].append(xbc3[:, lp - (SSM_CONV_WIDTH - 1):, :].astype(x_prompt.dtype))
        outs[2].append(ssm_t.reshape(bp, SSM_HEADS, SSM_HEAD_DIM, D_STATE).astype(state_ssm.dtype))

        s1 = _ffn(ysm, *f1)
        su, sz, sxbc, sdt = _inproj(s1, row(mix_norm[i]), w_main, w_dt)
        tap_major = lambda st: jnp.transpose(st.astype(f32), (1, 0, 2))
        new_a, sua = _sample_conv_a(tap_major(state_conv_a[i]), su, p)
        new_b, xs, xt, dat, b_mat, ct = _sample_conv_b(tap_major(state_conv_b[i]), sxbc, sdt, p)
        new_s, yt = _sample_ssd(state_ssm[i].astype(f32), xt, dat, b_mat, ct)
        sya, sys_ = _sample_gate(sua, yt, xs, sz, p)
        ysm = _ffn(s1, *f2, ya=sya, ys=sys_, wo=wo, final_g=fin)
        outs[3].append(jnp.transpose(new_a, (1, 0, 2)).astype(x_sample.dtype))
        outs[4].append(jnp.transpose(new_b, (1, 0, 2)).astype(x_sample.dtype))
        outs[5].append(new_s.astype(state_ssm.dtype))

    return (yp.reshape(bp, lp, D_MODEL), ysm.reshape(bs, 1, D_MODEL)) + tuple(jnp.stack(o) for o in outs)
```

```python
import functools

import jax
import jax.numpy as jnp
from jax import lax
from jax.experimental import pallas as pl
from jax.experimental.pallas import tpu as pltpu

D_MODEL = 1024
D_FF = 2816
D_CONV = 1024
CONV_A_WIDTH = 31
D_SSM = 1024
SSM_HEAD_DIM = 64
SSM_HEADS = 16
SSM_GROUPS = 2
HEADS_PER_GROUP = SSM_HEADS // SSM_GROUPS
GROUP_WIDTH = HEADS_PER_GROUP * SSM_HEAD_DIM
D_STATE = 128
SSM_CONV_WIDTH = 4
CHUNK = 128
D_XBC = D_SSM + 2 * SSM_GROUPS * D_STATE
D_BC = SSM_GROUPS * D_STATE
FFN_RES_WEIGHT = 0.5
NORM_EPS = 1e-5

LANES = 128
SUBLANES = 8
DT_PAD = LANES
A_TAIL = 32
B_TAIL = 8
ROW_TILE = 512
CONV_ROWS = 128
SSD_CHUNKS_PER_STEP = 8
SAMPLE_SSD_BLOCK = 8
SAMPLE_CONV_LANES = 256
CAST_T_BLOCK_ROWS = 256
VMEM_LIMIT = 56 * 1024 * 1024
NEG_BIG = -1e30

ACT_DTYPE = jnp.bfloat16
MXU_DTYPE = jnp.bfloat16


def _dot(a, b):
    return jnp.dot(a.astype(MXU_DTYPE), b.astype(MXU_DTYPE), preferred_element_type=jnp.float32)


def _dot_nt(a, b):
    return lax.dot_general(a.astype(MXU_DTYPE), b.astype(MXU_DTYPE), (((1,), (1,)), ((), ())),
                           preferred_element_type=jnp.float32)


def _split2(v):
    hi = v.astype(MXU_DTYPE)
    lo = (v - hi.astype(jnp.float32)).astype(MXU_DTYPE)
    return hi, lo


def _dot_split_lhs(v, m):
    hi, lo = _split2(v)
    return (jnp.dot(hi, m, preferred_element_type=jnp.float32)
            + jnp.dot(lo, m, preferred_element_type=jnp.float32))


def _dot_split_rhs(m, v):
    hi = v.astype(MXU_DTYPE)
    r1 = v - hi.astype(jnp.float32)
    mid = r1.astype(MXU_DTYPE)
    lo = (r1 - mid.astype(jnp.float32)).astype(MXU_DTYPE)
    return (jnp.dot(m, hi, preferred_element_type=jnp.float32)
            + jnp.dot(m, mid, preferred_element_type=jnp.float32)
            + jnp.dot(m, lo, preferred_element_type=jnp.float32))


def _rms_norm(x, g):
    return x * lax.rsqrt(jnp.mean(x * x, axis=-1, keepdims=True) + NORM_EPS) * g


def _silu(x):
    return x * jax.nn.sigmoid(x)


def _softplus(x):
    return jnp.maximum(x, 0.0) + jnp.log1p(jnp.exp(-jnp.abs(x)))


def _layer_norm_silu(x, g, b):
    mu = jnp.mean(x, axis=-1, keepdims=True)
    xc = x - mu
    var = jnp.mean(xc * xc, axis=-1, keepdims=True)
    return _silu(xc * lax.rsqrt(var + NORM_EPS) * g + b)


def _gate_group_norm(y, z, g):
    y = y * _silu(z)
    parts = []
    for i in range(SSM_GROUPS):
        yg = y[:, i * GROUP_WIDTH:(i + 1) * GROUP_WIDTH]
        parts.append(yg * lax.rsqrt(jnp.mean(yg * yg, axis=-1, keepdims=True) + NORM_EPS))
    return jnp.concatenate(parts, axis=-1) * g


def _causal_conv_tile(ext_ref, w_ref, lanes, row0, n_out, tail, width):
    n_rows = tail + n_out
    x = ext_ref[row0:row0 + n_rows, lanes]
    first = tail - (width - 1)
    acc = jnp.zeros((n_out, LANES), jnp.float32)
    for r in range(SUBLANES):
        taps = [k for k in range(width) if (first + k) % SUBLANES == r]
        if not taps:
            continue
        xr = x if r == 0 else pltpu.roll(x, n_rows - r, 0)
        for k in taps:
            base = first + k - r
            acc = acc + xr[base:base + n_out, :] * w_ref[k:k + 1, lanes]
    return acc


def _const_spec(shape):
    nd = len(shape)
    return pl.BlockSpec(shape, lambda *_: (0,) * nd, pipeline_mode=pl.Buffered(1))


def _full_spec(shape):
    nd = len(shape)
    return pl.BlockSpec(shape, lambda *_: (0,) * nd)


def _params(*semantics):
    return pltpu.CompilerParams(dimension_semantics=semantics, vmem_limit_bytes=VMEM_LIMIT)


def _ffn_kernel(*refs, with_outproj, with_final_norm, n_cast, n_cast_t):
    refs = list(refs)
    x_ref = refs.pop(0)
    if with_outproj:
        ya_ref, ys_ref, wo_ref = refs[:3]
        refs = refs[3:]
    g_ref, wg_ref, wu_ref, wd_ref = refs[:4]
    refs = refs[4:]
    if with_final_norm:
        fg_ref = refs.pop(0)
    n_jobs = n_cast + n_cast_t
    cast_in = refs[:n_jobs]
    o_ref = refs[n_jobs]
    cast_out = refs[n_jobs + 1:]
    for src, dst in zip(cast_in[:n_cast], cast_out[:n_cast]):
        dst[...] = src[...].astype(dst.dtype)
    for src, dst in zip(cast_in[n_cast:], cast_out[n_cast:]):
        for a in range(src.shape[0] // LANES):
            for b in range(src.shape[1] // LANES):
                dst[b * LANES:(b + 1) * LANES, a * LANES:(a + 1) * LANES] = (
                    src[a * LANES:(a + 1) * LANES, b * LANES:(b + 1) * LANES].T.astype(dst.dtype))

    x = x_ref[...]
    if with_outproj:
        x = x + _dot(ya_ref[...], wo_ref[0:D_CONV, :]) + _dot(ys_ref[...], wo_ref[D_CONV:D_CONV + D_SSM, :])
    xn = _rms_norm(x, g_ref[...]).astype(MXU_DTYPE)
    gate = jnp.dot(xn, wg_ref[...], preferred_element_type=jnp.float32)
    up = jnp.dot(xn, wu_ref[...], preferred_element_type=jnp.float32)
    h = (_silu(gate) * up).astype(MXU_DTYPE)
    y = x + FFN_RES_WEIGHT * jnp.dot(h, wd_ref[...], preferred_element_type=jnp.float32)
    if with_final_norm:
        y = _rms_norm(y, fg_ref[...])
    o_ref[...] = y


def _cast_block_rows(n_rows, n_steps):
    tile = 2 * SUBLANES
    n_blocks = max(d for d in range(1, n_steps + 1) if n_rows % d == 0 and (n_rows // d) % tile == 0)
    return n_rows // n_blocks


def _ffn(x, norm_g, wg, wu, wd, ya=None, ys=None, wo=None, final_g=None, cast=(), cast_t=()):
    m = x.shape[0]
    tm = min(m, ROW_TILE)
    n_steps = m // tm
    with_outproj = ya is not None
    with_final_norm = final_g is not None
    row = lambda i: (i, 0)
    args, specs = [x], [pl.BlockSpec((tm, D_MODEL), row)]
    if with_outproj:
        args += [ya, ys, wo]
        specs += [pl.BlockSpec((tm, D_CONV), row), pl.BlockSpec((tm, D_SSM), row), _const_spec(wo.shape)]
    args += [norm_g, wg, wu, wd]
    specs += [_const_spec(norm_g.shape), _const_spec(wg.shape), _const_spec(wu.shape), _const_spec(wd.shape)]
    if with_final_norm:
        args.append(final_g)
        specs.append(_const_spec(final_g.shape))
    cast_specs = []
    for w in cast:
        br = _cast_block_rows(w.shape[0], n_steps)
        last = w.shape[0] // br - 1
        cast_specs.append(pl.BlockSpec((br, w.shape[1]), lambda i, last=last: (jnp.minimum(i, last), 0)))
    out_specs = list(cast_specs)
    out_shapes = [jax.ShapeDtypeStruct(w.shape, MXU_DTYPE) for w in cast]
    for w, n_rows in cast_t:
        br = CAST_T_BLOCK_ROWS
        last = n_rows // br - 1
        assert n_rows % br == 0 and last < n_steps and w.shape[1] % LANES == 0
        cast_specs.append(pl.BlockSpec((br, w.shape[1]), lambda i, last=last: (jnp.minimum(i, last), 0)))
        out_specs.append(pl.BlockSpec((w.shape[1], br), lambda i, last=last: (0, jnp.minimum(i, last))))
        out_shapes.append(jax.ShapeDtypeStruct((w.shape[1], n_rows), MXU_DTYPE))
    out = pl.pallas_call(
        functools.partial(_ffn_kernel, with_outproj=with_outproj, with_final_norm=with_final_norm,
                          n_cast=len(cast), n_cast_t=len(cast_t)),
        grid=(n_steps,),
        in_specs=specs + cast_specs,
        out_specs=[pl.BlockSpec((tm, D_MODEL), row)] + out_specs,
        out_shape=[jax.ShapeDtypeStruct((m, D_MODEL), jnp.float32)] + out_shapes,
        compiler_params=_params("arbitrary"),
        name="ffn_out" if with_outproj else "ffn_in",
    )(*args, *cast, *[w for w, _ in cast_t])
    return out if (cast or cast_t) else out[0]


def _project(x_ref, g_ref, w_ref, wdt_ref, z_ref, xbc_ref, dt_ref):
    xn = _rms_norm(x_ref[...], g_ref[...]).astype(MXU_DTYPE)

    def proj(lo, width):
        return jnp.dot(xn, w_ref[:, lo:lo + width], preferred_element_type=jnp.float32)

    u = proj(0, D_CONV) * jax.nn.sigmoid(proj(D_CONV, D_CONV))
    z_ref[...] = proj(2 * D_CONV, D_SSM).astype(z_ref.dtype)
    xbc_ref[...] = proj(2 * D_CONV + D_SSM, D_XBC).astype(xbc_ref.dtype)
    dt_ref[...] = jnp.dot(xn, wdt_ref[...], preferred_element_type=jnp.float32)
    return u


def _inproj_kernel(x_ref, g_ref, w_ref, wdt_ref, u_ref, z_ref, xbc_ref, dt_ref):
    u_ref[...] = _project(x_ref, g_ref, w_ref, wdt_ref, z_ref, xbc_ref, dt_ref)


def _inproj_conv_kernel(x_ref, g_ref, w_ref, wdt_ref, cwa_ref, cba_ref, lng_ref, lnb_ref,
                        ya_ref, tail_ref, z_ref, xbc_ref, dt_ref, ext_ref, ua_ref, *, tiles_per_seq):
    tm = x_ref.shape[0]

    @pl.when(pl.program_id(0) % tiles_per_seq == 0)
    def _():
        ext_ref[0:A_TAIL, :] = jnp.zeros((A_TAIL, D_CONV), jnp.float32)

    ext_ref[A_TAIL:A_TAIL + tm, :] = _project(x_ref, g_ref, w_ref, wdt_ref, z_ref, xbc_ref, dt_ref)
    for c in range(tm // CONV_ROWS):
        for j in range(D_CONV // LANES):
            lanes = slice(j * LANES, (j + 1) * LANES)
            ua_ref[c * CONV_ROWS:(c + 1) * CONV_ROWS, lanes] = (
                _causal_conv_tile(ext_ref, cwa_ref, lanes, c * CONV_ROWS, CONV_ROWS, A_TAIL, CONV_A_WIDTH)
                + cba_ref[:, lanes])
    tail = ext_ref[tm:tm + A_TAIL, :]
    tail_ref[0] = tail
    ext_ref[0:A_TAIL, :] = tail
    ya_ref[...] = _layer_norm_silu(ua_ref[...], lng_ref[...], lnb_ref[...]).astype(ya_ref.dtype)


def _inproj(x, norm_g, w_main, w_dt):
    m = x.shape[0]
    tm = min(m, ROW_TILE)
    row = lambda i: (i, 0)
    widths = (D_CONV, D_SSM, D_XBC, DT_PAD)
    dtypes = (jnp.float32, ACT_DTYPE, ACT_DTYPE, jnp.float32)
    return pl.pallas_call(
        _inproj_kernel,
        grid=(m // tm,),
        in_specs=[pl.BlockSpec((tm, D_MODEL), row), _const_spec(norm_g.shape), _const_spec(w_main.shape),
                  _const_spec(w_dt.shape)],
        out_specs=[pl.BlockSpec((tm, w), row) for w in widths],
        out_shape=[jax.ShapeDtypeStruct((m, w), d) for w, d in zip(widths, dtypes)],
        compiler_params=_params("arbitrary"),
        name="inproj",
    )(x, norm_g, w_main, w_dt)


def _inproj_conv(x, norm_g, w_main, w_dt, conv, seq_len):
    m = x.shape[0]
    tm = min(seq_len, ROW_TILE)
    tiles_per_seq = seq_len // tm
    f32 = jnp.float32
    row = lambda i: (i, 0)
    return pl.pallas_call(
        functools.partial(_inproj_conv_kernel, tiles_per_seq=tiles_per_seq),
        grid=(m // tm,),
        in_specs=[pl.BlockSpec((tm, D_MODEL), row), _const_spec(norm_g.shape), _const_spec(w_main.shape),
                  _const_spec(w_dt.shape)] + [_const_spec(a.shape) for a in conv],
        out_specs=[pl.BlockSpec((tm, D_CONV), row),
                   pl.BlockSpec((1, A_TAIL, D_CONV), lambda i: (i // tiles_per_seq, 0, 0))]
                  + [pl.BlockSpec((tm, w), row) for w in (D_SSM, D_XBC, DT_PAD)],
        out_shape=[jax.ShapeDtypeStruct((m, D_CONV), ACT_DTYPE),
                   jax.ShapeDtypeStruct((m // seq_len, A_TAIL, D_CONV), f32),
                   jax.ShapeDtypeStruct((m, D_SSM), ACT_DTYPE), jax.ShapeDtypeStruct((m, D_XBC), ACT_DTYPE),
                   jax.ShapeDtypeStruct((m, DT_PAD), f32)],
        scratch_shapes=[pltpu.VMEM((A_TAIL + tm, D_CONV), f32), pltpu.VMEM((tm, D_CONV), f32)],
        compiler_params=_params("arbitrary"),
        name="inproj_conv",
    )(x, norm_g, w_main, w_dt, *conv)


def _ssd_kernel(z_ref, xbc_ref, dt_ref,
                cwb_ref, cbb_ref, dtb_ref, a_ref, dexp_ref, nrm_ref, e_ref,
                ys_ref, ssm_ref,
                extb_ref, ht_ref, xc_ref, y_ref):
    c = pl.program_id(1)
    f32 = jnp.float32
    n_rows = SSD_CHUNKS_PER_STEP * CHUNK

    @pl.when(c == 0)
    def _():
        extb_ref[0:B_TAIL, :] = jnp.zeros((B_TAIL, D_XBC), f32)
        ht_ref[...] = jnp.zeros_like(ht_ref)

    extb_ref[B_TAIL:B_TAIL + n_rows, :] = xbc_ref[0].astype(f32)
    for q in range(SSD_CHUNKS_PER_STEP):
        for j in range(D_XBC // LANES):
            lanes = slice(j * LANES, (j + 1) * LANES)
            xc_ref[q * CHUNK:(q + 1) * CHUNK, lanes] = _silu(
                _causal_conv_tile(extb_ref, cwb_ref, lanes, q * CHUNK, CHUNK, B_TAIL, SSM_CONV_WIDTH)
                + cbb_ref[:, lanes])
    extb_ref[0:B_TAIL, :] = extb_ref[n_rows:n_rows + B_TAIL, :]

    rows = lax.broadcasted_iota(jnp.int32, (CHUNK, CHUNK), 0)
    cols = lax.broadcasted_iota(jnp.int32, (CHUNK, CHUNK), 1)
    causal = rows >= cols
    tri = causal.astype(MXU_DTYPE)
    lane = lax.broadcasted_iota(jnp.int32, (CHUNK, LANES), 1)
    e_mat = e_ref[...]
    gw = GROUP_WIDTH
    for q in range(SSD_CHUNKS_PER_STEP):
        rs = slice(q * CHUNK, (q + 1) * CHUNK)
        dt = _softplus(dt_ref[0, rs, :] + dtb_ref[...])
        a = dt * a_ref[...]
        a_cs = _dot_split_rhs(tri, a)
        a_cs_t = a_cs.T
        dt_t = dt.T
        ea = jnp.exp(a_cs)
        w_end = jnp.exp(a_cs[CHUNK - 1:CHUNK, :] - a_cs) * dt
        w_exp = _dot_split_lhs(w_end, e_mat)
        ea_exp = _dot_split_lhs(ea, e_mat)
        xs = xc_ref[rs, 0:D_SSM]
        xs_b = xs.astype(MXU_DTYPE)
        xw_b = (xs * w_exp).astype(MXU_DTYPE)
        for g in range(SSM_GROUPS):
            b_g = xc_ref[rs, D_SSM + g * D_STATE:D_SSM + (g + 1) * D_STATE]
            c_g = xc_ref[rs, D_SSM + D_BC + g * D_STATE:D_SSM + D_BC + (g + 1) * D_STATE]
            c_gb = c_g.astype(MXU_DTYPE)
            cb = _dot_nt(c_gb, b_g)
            ht_g = ht_ref[:, g * gw:(g + 1) * gw]
            y_off = _dot(c_gb, ht_g) * ea_exp[:, g * gw:(g + 1) * gw]
            for hp in range(HEADS_PER_GROUP // 2):
                res = []
                for e in range(2):
                    h = g * HEADS_PER_GROUP + 2 * hp + e
                    seg = a_cs[:, h:h + 1] - a_cs_t[h:h + 1, :]
                    l_mat = jnp.exp(jnp.where(causal, seg, NEG_BIG))
                    gmat = (cb * l_mat * dt_t[h:h + 1, :]).astype(MXU_DTYPE)
                    lo = g * gw + hp * LANES
                    res.append(jnp.dot(gmat, xs_b[:, lo:lo + LANES], preferred_element_type=f32))
                y_pair = jnp.where(lane < SSM_HEAD_DIM, res[0], res[1])
                y_ref[rs, lo:lo + LANES] = y_pair + y_off[:, hp * LANES:(hp + 1) * LANES]
            s_loc = _dot(b_g.T, xw_b[:, g * gw:(g + 1) * gw])
            ht_ref[:, g * gw:(g + 1) * gw] = ht_g * ea_exp[CHUNK - 1:CHUNK, g * gw:(g + 1) * gw] + s_loc

        y = y_ref[rs, :] + dexp_ref[...] * xs
        ys_ref[0, rs, :] = _gate_group_norm(y, z_ref[0, rs, :].astype(f32), nrm_ref[...]).astype(ys_ref.dtype)

    @pl.when(c == pl.num_programs(1) - 1)
    def _():
        for j in range(D_SSM // LANES):
            ssm_ref[0, j * LANES:(j + 1) * LANES, :] = ht_ref[:, j * LANES:(j + 1) * LANES].T


def _ssd(z, xbc, dt_raw, p):
    b, l, _ = z.shape
    n_rows = SSD_CHUNKS_PER_STEP * CHUNK
    tok = lambda i, c: (i, c, 0)
    consts = [p["cwb"], p["cbb"], p["dtb"], p["a"], p["dexp"], p["nrm"], p["e"]]
    f32 = jnp.float32
    return pl.pallas_call(
        _ssd_kernel,
        grid=(b, l // n_rows),
        in_specs=[pl.BlockSpec((1, n_rows, D_SSM), tok), pl.BlockSpec((1, n_rows, D_XBC), tok),
                  pl.BlockSpec((1, n_rows, DT_PAD), tok)] + [_const_spec(a.shape) for a in consts],
        out_specs=[pl.BlockSpec((1, n_rows, D_SSM), tok),
                   pl.BlockSpec((1, D_SSM, D_STATE), lambda i, c: (i, 0, 0))],
        out_shape=[jax.ShapeDtypeStruct((b, l, D_SSM), ACT_DTYPE),
                   jax.ShapeDtypeStruct((b, D_SSM, D_STATE), f32)],
        scratch_shapes=[pltpu.VMEM((B_TAIL + n_rows, D_XBC), f32),
                        pltpu.VMEM((D_STATE, D_SSM), f32),
                        pltpu.VMEM((n_rows, D_XBC), f32),
                        pltpu.VMEM((n_rows, D_SSM), f32)],
        compiler_params=_params("arbitrary", "arbitrary"),
        name="ssd",
    )(z, xbc, dt_raw, *consts)


def _sample_conv_a_kernel(st_ref, u_ref, w_ref, cba_ref, new_ref, ua_ref):
    hist = CONV_A_WIDTH - 1
    for j in range(u_ref.shape[1] // LANES):
        lanes = slice(j * LANES, (j + 1) * LANES)
        u = u_ref[:, lanes]
        acc = u * w_ref[hist:hist + 1, lanes] + cba_ref[:, lanes]
        for k in range(hist):
            row = st_ref[k, :, lanes]
            acc = acc + row * w_ref[k:k + 1, lanes]
            if k >= 1:
                new_ref[k - 1, :, lanes] = row
        new_ref[hist - 1, :, lanes] = u
        ua_ref[:, lanes] = acc


def _sample_conv_a(state_k, u, p):
    hist, s, _ = state_k.shape
    wb = SAMPLE_CONV_LANES
    f32 = jnp.float32
    blk3 = pl.BlockSpec((hist, s, wb), lambda j: (0, 0, j))
    blk2 = pl.BlockSpec((s, wb), lambda j: (0, j))
    return pl.pallas_call(
        _sample_conv_a_kernel,
        grid=(D_CONV // wb,),
        in_specs=[blk3, blk2, pl.BlockSpec((A_TAIL, wb), lambda j: (0, j)), pl.BlockSpec((1, wb), lambda j: (0, j))],
        out_specs=[blk3, blk2],
        out_shape=[jax.ShapeDtypeStruct((hist, s, D_CONV), f32), jax.ShapeDtypeStruct((s, D_CONV), f32)],
        compiler_params=_params("arbitrary"),
        name="sample_conv_a",
    )(state_k, u, p["cwa"], p["cba"])


def _sample_conv_b_kernel(st_ref, xbc_ref, dt_ref, cwb_ref, cbb_ref, dtb_ref, a_ref, e_ref,
                          new_ref, xs_ref, xt_ref, dat_ref, b_ref, ct_ref):
    hist = SSM_CONV_WIDTH - 1
    xbc = xbc_ref[...].astype(jnp.float32)
    acc = xbc * cwb_ref[hist:hist + 1, :] + cbb_ref[...]
    for k in range(hist):
        acc = acc + st_ref[k] * cwb_ref[k:k + 1, :]
    for k in range(hist - 1):
        new_ref[k] = st_ref[k + 1]
    new_ref[hist - 1] = xbc
    xc = _silu(acc)
    xs = xc[:, 0:D_SSM]
    dt = _softplus(dt_ref[...] + dtb_ref[...])
    da = jnp.exp(dt * a_ref[...])
    xs_ref[...] = xs
    x_dt = xs * _dot_split_lhs(dt, e_ref[...])
    for j in range(D_SSM // LANES):
        xt_ref[j * LANES:(j + 1) * LANES, :] = x_dt[:, j * LANES:(j + 1) * LANES].T.astype(xt_ref.dtype)
    dat_ref[...] = da.T[0:SSM_HEADS, :]
    b_ref[...] = xc[:, D_SSM:D_SSM + D_BC]
    for g in range(SSM_GROUPS):
        lo = D_SSM + D_BC + g * D_STATE
        ct_ref[g * D_STATE:(g + 1) * D_STATE, :] = xc[:, lo:lo + D_STATE].T.astype(ct_ref.dtype)


def _sample_conv_b(state_k, xbc, dt_raw, p):
    hist, s, _ = state_k.shape
    f32 = jnp.float32
    args = [state_k, xbc, dt_raw, p["cwb"], p["cbb"], p["dtb"], p["a"], p["e"]]
    shapes = [((hist, s, D_XBC), f32), ((s, D_SSM), f32), ((D_SSM, s), MXU_DTYPE), ((SSM_HEADS, s), f32),
              ((s, D_BC), f32), ((D_BC, s), MXU_DTYPE)]
    return pl.pallas_call(
        _sample_conv_b_kernel,
        grid=(1,),
        in_specs=[_const_spec(a.shape) for a in args],
        out_specs=[_full_spec(sh) for sh, _ in shapes],
        out_shape=[jax.ShapeDtypeStruct(sh, d) for sh, d in shapes],
        compiler_params=_params("arbitrary"),
        name="sample_conv_b",
    )(*args)


def _sample_ssd_kernel(dat_ref, st_ref, xt_ref, b_ref, ct_ref, new_ref, yt_ref):
    i = pl.program_id(0)
    bs = st_ref.shape[0]
    n_seq = yt_ref.shape[1]
    f32 = jnp.float32

    @pl.when(i == 0)
    def _():
        yt_ref[...] = jnp.zeros_like(yt_ref)

    row_id = lax.broadcasted_iota(jnp.int32, (n_seq, D_STATE), 0)
    col_id = lax.broadcasted_iota(jnp.int32, (GROUP_WIDTH, n_seq), 1)

    def body(sl, carry):
        s = i * bs + sl
        for g in range(SSM_GROUPS):
            rows = slice(g * GROUP_WIDTH, (g + 1) * GROUP_WIDTH)
            b_sel = jnp.where(row_id == s, b_ref[:, g * D_STATE:(g + 1) * D_STATE], 0.0).astype(MXU_DTYPE)
            outer = jnp.dot(xt_ref[rows, :], b_sel, preferred_element_type=f32)
            h_new = []
            for hh in range(HEADS_PER_GROUP):
                h = g * HEADS_PER_GROUP + hh
                hn = st_ref[sl, h] * dat_ref[h, s] + outer[hh * SSM_HEAD_DIM:(hh + 1) * SSM_HEAD_DIM, :]
                new_ref[sl, h] = hn
                h_new.append(hn.astype(MXU_DTYPE))
            y_all = jnp.dot(jnp.concatenate(h_new, axis=0), ct_ref[g * D_STATE:(g + 1) * D_STATE, :],
                            preferred_element_type=f32)
            yt_ref[rows, :] = jnp.where(col_id == s, y_all, yt_ref[rows, :])
        return carry

    lax.fori_loop(0, bs, body, 0, unroll=2)


def _sample_ssd(state, xt, dat, b, ct):
    s = state.shape[0]
    bs = SAMPLE_SSD_BLOCK
    blk = pl.BlockSpec((bs, SSM_HEADS, SSM_HEAD_DIM, D_STATE), lambda i: (i, 0, 0, 0))
    return pl.pallas_call(
        _sample_ssd_kernel,
        grid=(s // bs,),
        in_specs=[pl.BlockSpec(memory_space=pltpu.SMEM), blk,
                  _const_spec(xt.shape), _const_spec(b.shape), _const_spec(ct.shape)],
        out_specs=[blk, _full_spec((D_SSM, s))],
        out_shape=[jax.ShapeDtypeStruct(state.shape, jnp.float32), jax.ShapeDtypeStruct((D_SSM, s), jnp.float32)],
        compiler_params=_params("arbitrary"),
        name="sample_ssd",
    )(dat, state, xt, b, ct)


def _sample_gate_kernel(ua_ref, lng_ref, lnb_ref, yt_ref, xs_ref, z_ref, dexp_ref, nrm_ref, ya_ref, ys_ref):
    ya_ref[...] = _layer_norm_silu(ua_ref[...], lng_ref[...], lnb_ref[...]).astype(ya_ref.dtype)
    y = jnp.concatenate([yt_ref[j * LANES:(j + 1) * LANES, :].T for j in range(D_SSM // LANES)], axis=-1)
    y = y + dexp_ref[...] * xs_ref[...]
    ys_ref[...] = _gate_group_norm(y, z_ref[...].astype(jnp.float32), nrm_ref[...]).astype(ys_ref.dtype)


def _sample_gate(ua, yt, xs, z, p):
    s = xs.shape[0]
    args = [ua, p["lng"], p["lnb"], yt, xs, z, p["dexp"], p["nrm"]]
    return pl.pallas_call(
        _sample_gate_kernel,
        grid=(1,),
        in_specs=[_const_spec(a.shape) for a in args],
        out_specs=[_full_spec((s, D_CONV)), _full_spec((s, D_SSM))],
        out_shape=[jax.ShapeDtypeStruct((s, D_CONV), ACT_DTYPE), jax.ShapeDtypeStruct((s, D_SSM), ACT_DTYPE)],
        compiler_params=_params("arbitrary"),
        name="sample_gate",
    )(*args)


def _prep_params(i, conv_dw_w, conv_dw_b, conv_ln_g, conv_ln_b, ssm_conv_w, ssm_conv_b, ssm_dt_bias, ssm_a_log,
                 ssm_d, ssm_norm):
    f32 = jnp.float32
    row = lambda v: v.reshape(1, -1).astype(f32)
    pad_heads = lambda v: jnp.pad(v.astype(f32), (0, DT_PAD - SSM_HEADS)).reshape(1, DT_PAD)
    head_of_channel = jnp.arange(D_SSM, dtype=jnp.int32) // SSM_HEAD_DIM
    expand = (jnp.arange(DT_PAD, dtype=jnp.int32)[:, None] == head_of_channel[None, :]).astype(MXU_DTYPE)
    return {
        "cwa": jnp.pad(conv_dw_w[i].astype(f32), ((0, A_TAIL - CONV_A_WIDTH), (0, 0))),
        "cba": row(conv_dw_b[i]), "lng": row(conv_ln_g[i]), "lnb": row(conv_ln_b[i]),
        "cwb": jnp.pad(ssm_conv_w[i].astype(f32), ((0, B_TAIL - SSM_CONV_WIDTH), (0, 0))),
        "cbb": row(ssm_conv_b[i]),
        "dtb": pad_heads(ssm_dt_bias[i]),
        "a": pad_heads(-jnp.exp(ssm_a_log[i].astype(f32))),
        "dexp": jnp.repeat(ssm_d[i].astype(f32), SSM_HEAD_DIM).reshape(1, D_SSM),
        "nrm": row(ssm_norm[i]),
        "e": expand,
    }


def kernel(x_prompt, x_sample, state_conv_a, state_conv_b, state_ssm, ffn1_norm, ffn1_w_gate, ffn1_w_up, ffn1_w_down, mix_norm, w_in, conv_dw_w, conv_dw_b, conv_ln_g, conv_ln_b, ssm_conv_w, ssm_conv_b, ssm_dt_bias, ssm_a_log, ssm_d, ssm_norm, w_out, ffn2_norm, ffn2_w_gate, ffn2_w_up, ffn2_w_down, final_norm):
    depth = ffn1_norm.shape[0]
    bp, lp, _ = x_prompt.shape
    bs = x_sample.shape[0]
    f32 = jnp.float32
    row = lambda v: v.reshape(1, -1).astype(f32)
    w16 = lambda w: w.astype(MXU_DTYPE)

    yp = x_prompt.reshape(bp * lp, D_MODEL)
    ysm = x_sample.reshape(bs, D_MODEL)
    outs = [[] for _ in range(6)]
    for i in range(depth):
        p = _prep_params(i, conv_dw_w, conv_dw_b, conv_ln_g, conv_ln_b, ssm_conv_w, ssm_conv_b, ssm_dt_bias,
                         ssm_a_log, ssm_d, ssm_norm)
        w_in_t = jnp.transpose(w_in[i])
        n_main = 2 * D_CONV + D_SSM + D_XBC
        w_dt = w16(jnp.pad(jnp.transpose(w_in_t[n_main:]), ((0, 0), (0, DT_PAD - SSM_HEADS))))
        f1 = (row(ffn1_norm[i]), w16(ffn1_w_gate[i]), w16(ffn1_w_up[i]), w16(ffn1_w_down[i]))
        fin = row(final_norm) if i == depth - 1 else None

        x1, w2g, w2u, w2d, wo, w_main = _ffn(
            yp, *f1, cast=(ffn2_w_gate[i], ffn2_w_up[i], ffn2_w_down[i], w_out[i]), cast_t=((w_in_t, n_main),))
        f2 = (row(ffn2_norm[i]), w2g, w2u, w2d)
        ya, a_tail, z, xbc, dt_raw = _inproj_conv(x1, row(mix_norm[i]), w_main, w_dt,
                                                  (p["cwa"], p["cba"], p["lng"], p["lnb"]), lp)
        xbc3 = xbc.reshape(bp, lp, D_XBC)
        ys, ssm_t = _ssd(z.reshape(bp, lp, D_SSM), xbc3, dt_raw.reshape(bp, lp, DT_PAD), p)
        yp = _ffn(x1, *f2, ya=ya, ys=ys.reshape(bp * lp, D_SSM), wo=wo, final_g=fin)
        outs[0].append(a_tail[:, A_TAIL - (CONV_A_WIDTH - 1):, :].astype(x_prompt.dtype))
        outs[1].append(xbc3[:, lp - (SSM_CONV_WIDTH - 1):, :].astype(x_prompt.dtype))
        outs[2].append(ssm_t.reshape(bp, SSM_HEADS, SSM_HEAD_DIM, D_STATE).astype(state_ssm.dtype))

        s1 = _ffn(ysm, *f1)
        su, sz, sxbc, sdt = _inproj(s1, row(mix_norm[i]), w_main, w_dt)
        tap_major = lambda st: jnp.transpose(st.astype(f32), (1, 0, 2))
        new_a, sua = _sample_conv_a(tap_major(state_conv_a[i]), su, p)
        new_b, xs, xt, dat, b_mat, ct = _sample_conv_b(tap_major(state_conv_b[i]), sxbc, sdt, p)
        new_s, yt = _sample_ssd(state_ssm[i].astype(f32), xt, dat, b_mat, ct)
        sya, sys_ = _sample_gate(sua, yt, xs, sz, p)
        ysm = _ffn(s1, *f2, ya=sya, ys=sys_, wo=wo, final_g=fin)
        outs[3].append(jnp.transpose(new_a, (1, 0, 2)).astype(x_sample.dtype))
        outs[4].append(jnp.transpose(new_b, (1, 0, 2)).astype(x_sample.dtype))
        outs[5].append(new_s.astype(state_ssm.dtype))

    return (yp.reshape(bp, lp, D_MODEL), ysm.reshape(bs, 1, D_MODEL)) + tuple(jnp.stack(o) for o in outs)
```

```python
import functools

import jax
import jax.numpy as jnp
from jax import lax
from jax.experimental import pallas as pl
from jax.experimental.pallas import tpu as pltpu

D_MODEL = 1024
D_FF = 2816
D_CONV = 1024
CONV_A_WIDTH = 31
D_SSM = 1024
SSM_HEAD_DIM = 64
SSM_HEADS = 16
SSM_GROUPS = 2
HEADS_PER_GROUP = SSM_HEADS // SSM_GROUPS
GROUP_WIDTH = HEADS_PER_GROUP * SSM_HEAD_DIM
D_STATE = 128
SSM_CONV_WIDTH = 4
CHUNK = 128
D_XBC = D_SSM + 2 * SSM_GROUPS * D_STATE
D_BC = SSM_GROUPS * D_STATE
FFN_RES_WEIGHT = 0.5
NORM_EPS = 1e-5

LANES = 128
SUBLANES = 8
DT_PAD = LANES
A_TAIL = 32
B_TAIL = 8
ROW_TILE = 512
CONV_ROWS = 128
SSD_CHUNKS_PER_STEP = 8
SAMPLE_SSD_BLOCK = 8
SAMPLE_CONV_LANES = 256
CAST_T_BLOCK_ROWS = 256
VMEM_LIMIT = 56 * 1024 * 1024
NEG_BIG = -1e30

ACT_DTYPE = jnp.bfloat16
MXU_DTYPE = jnp.bfloat16


def _dot(a, b):
    return jnp.dot(a.astype(MXU_DTYPE), b.astype(MXU_DTYPE), preferred_element_type=jnp.float32)


def _dot_nt(a, b):
    return lax.dot_general(a.astype(MXU_DTYPE), b.astype(MXU_DTYPE), (((1,), (1,)), ((), ())),
                           preferred_element_type=jnp.float32)


def _split2(v):
    hi = v.astype(MXU_DTYPE)
    lo = (v - hi.astype(jnp.float32)).astype(MXU_DTYPE)
    return hi, lo


def _dot_split_lhs(v, m):
    hi, lo = _split2(v)
    return (jnp.dot(hi, m, preferred_element_type=jnp.float32)
            + jnp.dot(lo, m, preferred_element_type=jnp.float32))


def _dot_split_rhs(m, v):
    hi = v.astype(MXU_DTYPE)
    r1 = v - hi.astype(jnp.float32)
    mid = r1.astype(MXU_DTYPE)
    lo = (r1 - mid.astype(jnp.float32)).astype(MXU_DTYPE)
    return (jnp.dot(m, hi, preferred_element_type=jnp.float32)
            + jnp.dot(m, mid, preferred_element_type=jnp.float32)
            + jnp.dot(m, lo, preferred_element_type=jnp.float32))


def _rms_norm(x, g):
    return x * lax.rsqrt(jnp.mean(x * x, axis=-1, keepdims=True) + NORM_EPS) * g


def _silu(x):
    return x * jax.nn.sigmoid(x)


def _softplus(x):
    return jnp.maximum(x, 0.0) + jnp.log1p(jnp.exp(-jnp.abs(x)))


def _layer_norm_silu(x, g, b):
    mu = jnp.mean(x, axis=-1, keepdims=True)
    xc = x - mu
    var = jnp.mean(xc * xc, axis=-1, keepdims=True)
    return _silu(xc * lax.rsqrt(var + NORM_EPS) * g + b)


def _gate_group_norm(y, z, g):
    y = y * _silu(z)
    parts = []
    for i in range(SSM_GROUPS):
        yg = y[:, i * GROUP_WIDTH:(i + 1) * GROUP_WIDTH]
        parts.append(yg * lax.rsqrt(jnp.mean(yg * yg, axis=-1, keepdims=True) + NORM_EPS))
    return jnp.concatenate(parts, axis=-1) * g


def _causal_conv_tile(ext_ref, w_ref, lanes, row0, n_out, tail, width):
    n_rows = tail + n_out
    x = ext_ref[row0:row0 + n_rows, lanes]
    first = tail - (width - 1)
    acc = jnp.zeros((n_out, LANES), jnp.float32)
    for r in range(SUBLANES):
        taps = [k for k in range(width) if (first + k) % SUBLANES == r]
        if not taps:
            continue
        xr = x if r == 0 else pltpu.roll(x, n_rows - r, 0)
        for k in taps:
            base = first + k - r
            acc = acc + xr[base:base + n_out, :] * w_ref[k:k + 1, lanes]
    return acc


def _const_spec(shape):
    nd = len(shape)
    return pl.BlockSpec(shape, lambda *_: (0,) * nd, pipeline_mode=pl.Buffered(1))


def _full_spec(shape):
    nd = len(shape)
    return pl.BlockSpec(shape, lambda *_: (0,) * nd)


def _params(*semantics):
    return pltpu.CompilerParams(dimension_semantics=semantics, vmem_limit_bytes=VMEM_LIMIT)


def _ffn_kernel(*refs, with_outproj, with_final_norm, n_cast, n_cast_t):
    refs = list(refs)
    x_ref = refs.pop(0)
    if with_outproj:
        ya_ref, ys_ref, wo_ref = refs[:3]
        refs = refs[3:]
    g_ref, wg_ref, wu_ref, wd_ref = refs[:4]
    refs = refs[4:]
    if with_final_norm:
        fg_ref = refs.pop(0)
    n_jobs = n_cast + n_cast_t
    cast_in = refs[:n_jobs]
    o_ref = refs[n_jobs]
    cast_out = refs[n_jobs + 1:]
    for src, dst in zip(cast_in[:n_cast], cast_out[:n_cast]):
        dst[...] = src[...].astype(dst.dtype)
    for src, dst in zip(cast_in[n_cast:], cast_out[n_cast:]):
        for a in range(src.shape[0] // LANES):
            for b in range(src.shape[1] // LANES):
                dst[b * LANES:(b + 1) * LANES, a * LANES:(a + 1) * LANES] = (
                    src[a * LANES:(a + 1) * LANES, b * LANES:(b + 1) * LANES].T.astype(dst.dtype))

    x = x_ref[...]
    if with_outproj:
        x = x + _dot(ya_ref[...], wo_ref[0:D_CONV, :]) + _dot(ys_ref[...], wo_ref[D_CONV:D_CONV + D_SSM, :])
    xn = _rms_norm(x, g_ref[...]).astype(MXU_DTYPE)
    gate = jnp.dot(xn, wg_ref[...], preferred_element_type=jnp.float32)
    up = jnp.dot(xn, wu_ref[...], preferred_element_type=jnp.float32)
    h = (_silu(gate) * up).astype(MXU_DTYPE)
    y = x + FFN_RES_WEIGHT * jnp.dot(h, wd_ref[...], preferred_element_type=jnp.float32)
    if with_final_norm:
        y = _rms_norm(y, fg_ref[...])
    o_ref[...] = y


def _cast_block_rows(n_rows, n_steps):
    tile = 2 * SUBLANES
    n_blocks = max(d for d in range(1, n_steps + 1) if n_rows % d == 0 and (n_rows // d) % tile == 0)
    return n_rows // n_blocks


def _ffn(x, norm_g, wg, wu, wd, ya=None, ys=None, wo=None, final_g=None, cast=(), cast_t=()):
    m = x.shape[0]
    tm = min(m, ROW_TILE)
    n_steps = m // tm
    with_outproj = ya is not None
    with_final_norm = final_g is not None
    row = lambda i: (i, 0)
    args, specs = [x], [pl.BlockSpec((tm, D_MODEL), row)]
    if with_outproj:
        args += [ya, ys, wo]
        specs += [pl.BlockSpec((tm, D_CONV), row), pl.BlockSpec((tm, D_SSM), row), _const_spec(wo.shape)]
    args += [norm_g, wg, wu, wd]
    specs += [_const_spec(norm_g.shape), _const_spec(wg.shape), _const_spec(wu.shape), _const_spec(wd.shape)]
    if with_final_norm:
        args.append(final_g)
        specs.append(_const_spec(final_g.shape))
    cast_specs = []
    for w in cast:
        br = _cast_block_rows(w.shape[0], n_steps)
        last = w.shape[0] // br - 1
        cast_specs.append(pl.BlockSpec((br, w.shape[1]), lambda i, last=last: (jnp.minimum(i, last), 0)))
    out_specs = list(cast_specs)
    out_shapes = [jax.ShapeDtypeStruct(w.shape, MXU_DTYPE) for w in cast]
    for w, n_rows in cast_t:
        br = CAST_T_BLOCK_ROWS
        last = n_rows // br - 1
        assert n_rows % br == 0 and last < n_steps and w.shape[1] % LANES == 0
        cast_specs.append(pl.BlockSpec((br, w.shape[1]), lambda i, last=last: (jnp.minimum(i, last), 0)))
        out_specs.append(pl.BlockSpec((w.shape[1], br), lambda i, last=last: (0, jnp.minimum(i, last))))
        out_shapes.append(jax.ShapeDtypeStruct((w.shape[1], n_rows), MXU_DTYPE))
    out = pl.pallas_call(
        functools.partial(_ffn_kernel, with_outproj=with_outproj, with_final_norm=with_final_norm,
                          n_cast=len(cast), n_cast_t=len(cast_t)),
        grid=(n_steps,),
        in_specs=specs + cast_specs,
        out_specs=[pl.BlockSpec((tm, D_MODEL), row)] + out_specs,
        out_shape=[jax.ShapeDtypeStruct((m, D_MODEL), jnp.float32)] + out_shapes,
        compiler_params=_params("arbitrary"),
        name="ffn_out" if with_outproj else "ffn_in",
    )(*args, *cast, *[w for w, _ in cast_t])
    return out if (cast or cast_t) else out[0]


def _project(x_ref, g_ref, w_ref, wdt_ref, z_ref, xbc_ref, dt_ref):
    xn = _rms_norm(x_ref[...], g_ref[...]).astype(MXU_DTYPE)

    def proj(lo, width):
        return jnp.dot(xn, w_ref[:, lo:lo + width], preferred_element_type=jnp.float32)

    u = proj(0, D_CONV) * jax.nn.sigmoid(proj(D_CONV, D_CONV))
    z_ref[...] = proj(2 * D_CONV, D_SSM).astype(z_ref.dtype)
    xbc_ref[...] = proj(2 * D_CONV + D_SSM, D_XBC).astype(xbc_ref.dtype)
    dt_ref[...] = _dot_nt(xn, wdt_ref[...])
    return u


def _inproj_kernel(x_ref, g_ref, w_ref, wdt_ref, u_ref, z_ref, xbc_ref, dt_ref):
    u_ref[...] = _project(x_ref, g_ref, w_ref, wdt_ref, z_ref, xbc_ref, dt_ref)


def _inproj_conv_kernel(x_ref, g_ref, w_ref, wdt_ref, cwa_ref, cba_ref, lng_ref, lnb_ref,
                        ya_ref, tail_ref, z_ref, xbc_ref, dt_ref, ext_ref, ua_ref, *, tiles_per_seq):
    tm = x_ref.shape[0]

    @pl.when(pl.program_id(0) % tiles_per_seq == 0)
    def _():
        ext_ref[0:A_TAIL, :] = jnp.zeros((A_TAIL, D_CONV), jnp.float32)

    ext_ref[A_TAIL:A_TAIL + tm, :] = _project(x_ref, g_ref, w_ref, wdt_ref, z_ref, xbc_ref, dt_ref)
    for c in range(tm // CONV_ROWS):
        for j in range(D_CONV // LANES):
            lanes = slice(j * LANES, (j + 1) * LANES)
            ua_ref[c * CONV_ROWS:(c + 1) * CONV_ROWS, lanes] = (
                _causal_conv_tile(ext_ref, cwa_ref, lanes, c * CONV_ROWS, CONV_ROWS, A_TAIL, CONV_A_WIDTH)
                + cba_ref[:, lanes])
    tail = ext_ref[tm:tm + A_TAIL, :]
    tail_ref[0] = tail
    ext_ref[0:A_TAIL, :] = tail
    ya_ref[...] = _layer_norm_silu(ua_ref[...], lng_ref[...], lnb_ref[...]).astype(ya_ref.dtype)


def _inproj(x, norm_g, w_main, w_dt):
    m = x.shape[0]
    tm = min(m, ROW_TILE)
    row = lambda i: (i, 0)
    widths = (D_CONV, D_SSM, D_XBC, DT_PAD)
    dtypes = (jnp.float32, ACT_DTYPE, ACT_DTYPE, jnp.float32)
    return pl.pallas_call(
        _inproj_kernel,
        grid=(m // tm,),
        in_specs=[pl.BlockSpec((tm, D_MODEL), row), _const_spec(norm_g.shape), _const_spec(w_main.shape),
                  _const_spec(w_dt.shape)],
        out_specs=[pl.BlockSpec((tm, w), row) for w in widths],
        out_shape=[jax.ShapeDtypeStruct((m, w), d) for w, d in zip(widths, dtypes)],
        compiler_params=_params("arbitrary"),
        name="inproj",
    )(x, norm_g, w_main, w_dt)


def _inproj_conv(x, norm_g, w_main, w_dt, conv, seq_len):
    m = x.shape[0]
    tm = min(seq_len, ROW_TILE)
    tiles_per_seq = seq_len // tm
    f32 = jnp.float32
    row = lambda i: (i, 0)
    return pl.pallas_call(
        functools.partial(_inproj_conv_kernel, tiles_per_seq=tiles_per_seq),
        grid=(m // tm,),
        in_specs=[pl.BlockSpec((tm, D_MODEL), row), _const_spec(norm_g.shape), _const_spec(w_main.shape),
                  _const_spec(w_dt.shape)] + [_const_spec(a.shape) for a in conv],
        out_specs=[pl.BlockSpec((tm, D_CONV), row),
                   pl.BlockSpec((1, A_TAIL, D_CONV), lambda i: (i // tiles_per_seq, 0, 0))]
                  + [pl.BlockSpec((tm, w), row) for w in (D_SSM, D_XBC, DT_PAD)],
        out_shape=[jax.ShapeDtypeStruct((m, D_CONV), ACT_DTYPE),
                   jax.ShapeDtypeStruct((m // seq_len, A_TAIL, D_CONV), f32),
                   jax.ShapeDtypeStruct((m, D_SSM), ACT_DTYPE), jax.ShapeDtypeStruct((m, D_XBC), ACT_DTYPE),
                   jax.ShapeDtypeStruct((m, DT_PAD), f32)],
        scratch_shapes=[pltpu.VMEM((A_TAIL + tm, D_CONV), f32), pltpu.VMEM((tm, D_CONV), f32)],
        compiler_params=_params("arbitrary"),
        name="inproj_conv",
    )(x, norm_g, w_main, w_dt, *conv)


def _ssd_kernel(z_ref, xbc_ref, dt_ref,
                cwb_ref, cbb_ref, dtb_ref, a_ref, dexp_ref, nrm_ref, e_ref,
                ys_ref, ssm_ref,
                extb_ref, ht_ref, xc_ref, y_ref):
    c = pl.program_id(1)
    f32 = jnp.float32
    n_rows = SSD_CHUNKS_PER_STEP * CHUNK

    @pl.when(c == 0)
    def _():
        extb_ref[0:B_TAIL, :] = jnp.zeros((B_TAIL, D_XBC), f32)
        ht_ref[...] = jnp.zeros_like(ht_ref)

    extb_ref[B_TAIL:B_TAIL + n_rows, :] = xbc_ref[0].astype(f32)
    for q in range(SSD_CHUNKS_PER_STEP):
        for j in range(D_XBC // LANES):
            lanes = slice(j * LANES, (j + 1) * LANES)
            xc_ref[q * CHUNK:(q + 1) * CHUNK, lanes] = _silu(
                _causal_conv_tile(extb_ref, cwb_ref, lanes, q * CHUNK, CHUNK, B_TAIL, SSM_CONV_WIDTH)
                + cbb_ref[:, lanes])
    extb_ref[0:B_TAIL, :] = extb_ref[n_rows:n_rows + B_TAIL, :]

    rows = lax.broadcasted_iota(jnp.int32, (CHUNK, CHUNK), 0)
    cols = lax.broadcasted_iota(jnp.int32, (CHUNK, CHUNK), 1)
    causal = rows >= cols
    tri = causal.astype(MXU_DTYPE)
    lane = lax.broadcasted_iota(jnp.int32, (CHUNK, LANES), 1)
    e_mat = e_ref[...]
    gw = GROUP_WIDTH
    for q in range(SSD_CHUNKS_PER_STEP):
        rs = slice(q * CHUNK, (q + 1) * CHUNK)
        dt = _softplus(dt_ref[0, rs, :] + dtb_ref[...])
        a = dt * a_ref[...]
        a_cs = _dot_split_rhs(tri, a)
        a_cs_t = a_cs.T
        dt_t = dt.T
        ea = jnp.exp(a_cs)
        w_end = jnp.exp(a_cs[CHUNK - 1:CHUNK, :] - a_cs) * dt
        w_exp = _dot_split_lhs(w_end, e_mat)
        ea_exp = _dot_split_lhs(ea, e_mat)
        xs = xc_ref[rs, 0:D_SSM]
        xs_b = xs.astype(MXU_DTYPE)
        xw_b = (xs * w_exp).astype(MXU_DTYPE)
        for g in range(SSM_GROUPS):
            b_g = xc_ref[rs, D_SSM + g * D_STATE:D_SSM + (g + 1) * D_STATE]
            c_g = xc_ref[rs, D_SSM + D_BC + g * D_STATE:D_SSM + D_BC + (g + 1) * D_STATE]
            c_gb = c_g.astype(MXU_DTYPE)
            cb = _dot_nt(c_gb, b_g)
            ht_g = ht_ref[:, g * gw:(g + 1) * gw]
            y_off = _dot(c_gb, ht_g) * ea_exp[:, g * gw:(g + 1) * gw]
            for hp in range(HEADS_PER_GROUP // 2):
                res = []
                for e in range(2):
                    h = g * HEADS_PER_GROUP + 2 * hp + e
                    seg = a_cs[:, h:h + 1] - a_cs_t[h:h + 1, :]
                    l_mat = jnp.exp(jnp.where(causal, seg, NEG_BIG))
                    gmat = (cb * l_mat * dt_t[h:h + 1, :]).astype(MXU_DTYPE)
                    lo = g * gw + hp * LANES
                    res.append(jnp.dot(gmat, xs_b[:, lo:lo + LANES], preferred_element_type=f32))
                y_pair = jnp.where(lane < SSM_HEAD_DIM, res[0], res[1])
                y_ref[rs, lo:lo + LANES] = y_pair + y_off[:, hp * LANES:(hp + 1) * LANES]
            s_loc = _dot(b_g.T, xw_b[:, g * gw:(g + 1) * gw])
            ht_ref[:, g * gw:(g + 1) * gw] = ht_g * ea_exp[CHUNK - 1:CHUNK, g * gw:(g + 1) * gw] + s_loc

        y = y_ref[rs, :] + dexp_ref[...] * xs
        ys_ref[0, rs, :] = _gate_group_norm(y, z_ref[0, rs, :].astype(f32), nrm_ref[...]).astype(ys_ref.dtype)

    @pl.when(c == pl.num_programs(1) - 1)
    def _():
        for j in range(D_SSM // LANES):
            ssm_ref[0, j * LANES:(j + 1) * LANES, :] = ht_ref[:, j * LANES:(j + 1) * LANES].T


def _ssd(z, xbc, dt_raw, p):
    b, l, _ = z.shape
    n_rows = SSD_CHUNKS_PER_STEP * CHUNK
    tok = lambda i, c: (i, c, 0)
    consts = [p["cwb"], p["cbb"], p["dtb"], p["a"], p["dexp"], p["nrm"], p["e"]]
    f32 = jnp.float32
    return pl.pallas_call(
        _ssd_kernel,
        grid=(b, l // n_rows),
        in_specs=[pl.BlockSpec((1, n_rows, D_SSM), tok), pl.BlockSpec((1, n_rows, D_XBC), tok),
                  pl.BlockSpec((1, n_rows, DT_PAD), tok)] + [_const_spec(a.shape) for a in consts],
        out_specs=[pl.BlockSpec((1, n_rows, D_SSM), tok),
                   pl.BlockSpec((1, D_SSM, D_STATE), lambda i, c: (i, 0, 0))],
        out_shape=[jax.ShapeDtypeStruct((b, l, D_SSM), ACT_DTYPE),
                   jax.ShapeDtypeStruct((b, D_SSM, D_STATE), f32)],
        scratch_shapes=[pltpu.VMEM((B_TAIL + n_rows, D_XBC), f32),
                        pltpu.VMEM((D_STATE, D_SSM), f32),
                        pltpu.VMEM((n_rows, D_XBC), f32),
                        pltpu.VMEM((n_rows, D_SSM), f32)],
        compiler_params=_params("arbitrary", "arbitrary"),
        name="ssd",
    )(z, xbc, dt_raw, *consts)


def _sample_conv_a_kernel(st_ref, u_ref, w_ref, cba_ref, new_ref, ua_ref):
    hist = CONV_A_WIDTH - 1
    for j in range(u_ref.shape[1] // LANES):
        lanes = slice(j * LANES, (j + 1) * LANES)
        u = u_ref[:, lanes]
        acc = u * w_ref[hist:hist + 1, lanes] + cba_ref[:, lanes]
        for k in range(hist):
            row = st_ref[k, :, lanes]
            acc = acc + row * w_ref[k:k + 1, lanes]
            if k >= 1:
                new_ref[k - 1, :, lanes] = row
        new_ref[hist - 1, :, lanes] = u
        ua_ref[:, lanes] = acc


def _sample_conv_a(state_k, u, p):
    hist, s, _ = state_k.shape
    wb = SAMPLE_CONV_LANES
    f32 = jnp.float32
    blk3 = pl.BlockSpec((hist, s, wb), lambda j: (0, 0, j))
    blk2 = pl.BlockSpec((s, wb), lambda j: (0, j))
    return pl.pallas_call(
        _sample_conv_a_kernel,
        grid=(D_CONV // wb,),
        in_specs=[blk3, blk2, pl.BlockSpec((A_TAIL, wb), lambda j: (0, j)), pl.BlockSpec((1, wb), lambda j: (0, j))],
        out_specs=[blk3, blk2],
        out_shape=[jax.ShapeDtypeStruct((hist, s, D_CONV), f32), jax.ShapeDtypeStruct((s, D_CONV), f32)],
        compiler_params=_params("arbitrary"),
        name="sample_conv_a",
    )(state_k, u, p["cwa"], p["cba"])


def _sample_conv_b_kernel(st_ref, xbc_ref, dt_ref, cwb_ref, cbb_ref, dtb_ref, a_ref, e_ref,
                          new_ref, xs_ref, xt_ref, dat_ref, b_ref, ct_ref):
    hist = SSM_CONV_WIDTH - 1
    xbc = xbc_ref[...].astype(jnp.float32)
    acc = xbc * cwb_ref[hist:hist + 1, :] + cbb_ref[...]
    for k in range(hist):
        acc = acc + st_ref[k] * cwb_ref[k:k + 1, :]
    for k in range(hist - 1):
        new_ref[k] = st_ref[k + 1]
    new_ref[hist - 1] = xbc
    xc = _silu(acc)
    xs = xc[:, 0:D_SSM]
    dt = _softplus(dt_ref[...] + dtb_ref[...])
    da = jnp.exp(dt * a_ref[...])
    xs_ref[...] = xs
    x_dt = xs * _dot_split_lhs(dt, e_ref[...])
    for j in range(D_SSM // LANES):
        xt_ref[j * LANES:(j + 1) * LANES, :] = x_dt[:, j * LANES:(j + 1) * LANES].T.astype(xt_ref.dtype)
    dat_ref[...] = da.T[0:SSM_HEADS, :]
    b_ref[...] = xc[:, D_SSM:D_SSM + D_BC]
    for g in range(SSM_GROUPS):
        lo = D_SSM + D_BC + g * D_STATE
        ct_ref[g * D_STATE:(g + 1) * D_STATE, :] = xc[:, lo:lo + D_STATE].T.astype(ct_ref.dtype)


def _sample_conv_b(state_k, xbc, dt_raw, p):
    hist, s, _ = state_k.shape
    f32 = jnp.float32
    args = [state_k, xbc, dt_raw, p["cwb"], p["cbb"], p["dtb"], p["a"], p["e"]]
    shapes = [((hist, s, D_XBC), f32), ((s, D_SSM), f32), ((D_SSM, s), MXU_DTYPE), ((SSM_HEADS, s), f32),
              ((s, D_BC), f32), ((D_BC, s), MXU_DTYPE)]
    return pl.pallas_call(
        _sample_conv_b_kernel,
        grid=(1,),
        in_specs=[_const_spec(a.shape) for a in args],
        out_specs=[_full_spec(sh) for sh, _ in shapes],
        out_shape=[jax.ShapeDtypeStruct(sh, d) for sh, d in shapes],
        compiler_params=_params("arbitrary"),
        name="sample_conv_b",
    )(*args)


def _sample_ssd_kernel(dat_ref, st_ref, xt_ref, b_ref, ct_ref, new_ref, yt_ref):
    i = pl.program_id(0)
    bs = st_ref.shape[0]
    n_seq = yt_ref.shape[1]
    f32 = jnp.float32

    @pl.when(i == 0)
    def _():
        yt_ref[...] = jnp.zeros_like(yt_ref)

    row_id = lax.broadcasted_iota(jnp.int32, (n_seq, D_STATE), 0)
    col_id = lax.broadcasted_iota(jnp.int32, (GROUP_WIDTH, n_seq), 1)

    def body(sl, carry):
        s = i * bs + sl
        for g in range(SSM_GROUPS):
            rows = slice(g * GROUP_WIDTH, (g + 1) * GROUP_WIDTH)
            b_sel = jnp.where(row_id == s, b_ref[:, g * D_STATE:(g + 1) * D_STATE], 0.0).astype(MXU_DTYPE)
            outer = jnp.dot(xt_ref[rows, :], b_sel, preferred_element_type=f32)
            h_new = []
            for hh in range(HEADS_PER_GROUP):
                h = g * HEADS_PER_GROUP + hh
                hn = st_ref[sl, h] * dat_ref[h, s] + outer[hh * SSM_HEAD_DIM:(hh + 1) * SSM_HEAD_DIM, :]
                new_ref[sl, h] = hn
                h_new.append(hn.astype(MXU_DTYPE))
            y_all = jnp.dot(jnp.concatenate(h_new, axis=0), ct_ref[g * D_STATE:(g + 1) * D_STATE, :],
                            preferred_element_type=f32)
            yt_ref[rows, :] = jnp.where(col_id == s, y_all, yt_ref[rows, :])
        return carry

    lax.fori_loop(0, bs, body, 0, unroll=2)


def _sample_ssd(state, xt, dat, b, ct):
    s = state.shape[0]
    bs = SAMPLE_SSD_BLOCK
    blk = pl.BlockSpec((bs, SSM_HEADS, SSM_HEAD_DIM, D_STATE), lambda i: (i, 0, 0, 0))
    return pl.pallas_call(
        _sample_ssd_kernel,
        grid=(s // bs,),
        in_specs=[pl.BlockSpec(memory_space=pltpu.SMEM), blk,
                  _const_spec(xt.shape), _const_spec(b.shape), _const_spec(ct.shape)],
        out_specs=[blk, _full_spec((D_SSM, s))],
        out_shape=[jax.ShapeDtypeStruct(state.shape, jnp.float32), jax.ShapeDtypeStruct((D_SSM, s), jnp.float32)],
        compiler_params=_params("arbitrary"),
        name="sample_ssd",
    )(dat, state, xt, b, ct)


def _sample_gate_kernel(ua_ref, lng_ref, lnb_ref, yt_ref, xs_ref, z_ref, dexp_ref, nrm_ref, ya_ref, ys_ref):
    ya_ref[...] = _layer_norm_silu(ua_ref[...], lng_ref[...], lnb_ref[...]).astype(ya_ref.dtype)
    y = jnp.concatenate([yt_ref[j * LANES:(j + 1) * LANES, :].T for j in range(D_SSM // LANES)], axis=-1)
    y = y + dexp_ref[...] * xs_ref[...]
    ys_ref[...] = _gate_group_norm(y, z_ref[...].astype(jnp.float32), nrm_ref[...]).astype(ys_ref.dtype)


def _sample_gate(ua, yt, xs, z, p):
    s = xs.shape[0]
    args = [ua, p["lng"], p["lnb"], yt, xs, z, p["dexp"], p["nrm"]]
    return pl.pallas_call(
        _sample_gate_kernel,
        grid=(1,),
        in_specs=[_const_spec(a.shape) for a in args],
        out_specs=[_full_spec((s, D_CONV)), _full_spec((s, D_SSM))],
        out_shape=[jax.ShapeDtypeStruct((s, D_CONV), ACT_DTYPE), jax.ShapeDtypeStruct((s, D_SSM), ACT_DTYPE)],
        compiler_params=_params("arbitrary"),
        name="sample_gate",
    )(*args)


def _prep_params(i, conv_dw_w, conv_dw_b, conv_ln_g, conv_ln_b, ssm_conv_w, ssm_conv_b, ssm_dt_bias, ssm_a_log,
                 ssm_d, ssm_norm):
    f32 = jnp.float32
    row = lambda v: v.reshape(1, -1).astype(f32)
    pad_heads = lambda v: jnp.pad(v.astype(f32), (0, DT_PAD - SSM_HEADS)).reshape(1, DT_PAD)
    head_of_channel = jnp.arange(D_SSM, dtype=jnp.int32) // SSM_HEAD_DIM
    expand = (jnp.arange(DT_PAD, dtype=jnp.int32)[:, None] == head_of_channel[None, :]).astype(MXU_DTYPE)
    return {
        "cwa": jnp.pad(conv_dw_w[i].astype(f32), ((0, A_TAIL - CONV_A_WIDTH), (0, 0))),
        "cba": row(conv_dw_b[i]), "lng": row(conv_ln_g[i]), "lnb": row(conv_ln_b[i]),
        "cwb": jnp.pad(ssm_conv_w[i].astype(f32), ((0, B_TAIL - SSM_CONV_WIDTH), (0, 0))),
        "cbb": row(ssm_conv_b[i]),
        "dtb": pad_heads(ssm_dt_bias[i]),
        "a": pad_heads(-jnp.exp(ssm_a_log[i].astype(f32))),
        "dexp": jnp.repeat(ssm_d[i].astype(f32), SSM_HEAD_DIM).reshape(1, D_SSM),
        "nrm": row(ssm_norm[i]),
        "e": expand,
    }


def kernel(x_prompt, x_sample, state_conv_a, state_conv_b, state_ssm, ffn1_norm, ffn1_w_gate, ffn1_w_up, ffn1_w_down, mix_norm, w_in, conv_dw_w, conv_dw_b, conv_ln_g, conv_ln_b, ssm_conv_w, ssm_conv_b, ssm_dt_bias, ssm_a_log, ssm_d, ssm_norm, w_out, ffn2_norm, ffn2_w_gate, ffn2_w_up, ffn2_w_down, final_norm):
    depth = ffn1_norm.shape[0]
    bp, lp, _ = x_prompt.shape
    bs = x_sample.shape[0]
    f32 = jnp.float32
    row = lambda v: v.reshape(1, -1).astype(f32)
    w16 = lambda w: w.astype(MXU_DTYPE)

    yp = x_prompt.reshape(bp * lp, D_MODEL)
    ysm = x_sample.reshape(bs, D_MODEL)
    outs = [[] for _ in range(6)]
    for i in range(depth):
        p = _prep_params(i, conv_dw_w, conv_dw_b, conv_ln_g, conv_ln_b, ssm_conv_w, ssm_conv_b, ssm_dt_bias,
                         ssm_a_log, ssm_d, ssm_norm)
        w_in_t = jnp.transpose(w_in[i])
        n_main = 2 * D_CONV + D_SSM + D_XBC
        w_dt = jnp.pad(w_in_t[n_main:], ((0, DT_PAD - SSM_HEADS), (0, 0)))
        f1 = (row(ffn1_norm[i]), w16(ffn1_w_gate[i]), w16(ffn1_w_up[i]), w16(ffn1_w_down[i]))
        fin = row(final_norm) if i == depth - 1 else None

        x1, w2g, w2u, w2d, wo, w_main = _ffn(
            yp, *f1, cast=(ffn2_w_gate[i], ffn2_w_up[i], ffn2_w_down[i], w_out[i]), cast_t=((w_in_t, n_main),))
        f2 = (row(ffn2_norm[i]), w2g, w2u, w2d)
        ya, a_tail, z, xbc, dt_raw = _inproj_conv(x1, row(mix_norm[i]), w_main, w_dt,
                                                  (p["cwa"], p["cba"], p["lng"], p["lnb"]), lp)
        xbc3 = xbc.reshape(bp, lp, D_XBC)
        ys, ssm_t = _ssd(z.reshape(bp, lp, D_SSM), xbc3, dt_raw.reshape(bp, lp, DT_PAD), p)
        yp = _ffn(x1, *f2, ya=ya, ys=ys.reshape(bp * lp, D_SSM), wo=wo, final_g=fin)
        outs[0].append(a_tail[:, A_TAIL - (CONV_A_WIDTH - 1):, :].astype(x_prompt.dtype))
        outs[1].append(xbc3[:, lp - (SSM_CONV_WIDTH - 1):, :].astype(x_prompt.dtype))
        outs[2].append(ssm_t.reshape(bp, SSM_HEADS, SSM_HEAD_DIM, D_STATE).astype(state_ssm.dtype))

        s1 = _ffn(ysm, *f1)
        su, sz, sxbc, sdt = _inproj(s1, row(mix_norm[i]), w_main, w_dt)
        tap_major = lambda st: jnp.transpose(st.astype(f32), (1, 0, 2))
        new_a, sua = _sample_conv_a(tap_major(state_conv_a[i]), su, p)
        new_b, xs, xt, dat, b_mat, ct = _sample_conv_b(tap_major(state_conv_b[i]), sxbc, sdt, p)
        new_s, yt = _sample_ssd(state_ssm[i].astype(f32), xt, dat, b_mat, ct)
        sya, sys_ = _sample_gate(sua, yt, xs, sz, p)
        ysm = _ffn(s1, *f2, ya=sya, ys=sys_, wo=wo, final_g=fin)
        outs[3].append(jnp.transpose(new_a, (1, 0, 2)).astype(x_sample.dtype))
        outs[4].append(jnp.transpose(new_b, (1, 0, 2)).astype(x_sample.dtype))
        outs[5].append(new_s.astype(state_ssm.dtype))

    return (yp.reshape(bp, lp, D_MODEL), ysm.reshape(bs, 1, D_MODEL)) + tuple(jnp.stack(o) for o in outs)
```

```python
import functools

import jax
import jax.numpy as jnp
from jax import lax
from jax.experimental import pallas as pl
from jax.experimental.pallas import tpu as pltpu

D_MODEL = 1024
D_FF = 2816
D_CONV = 1024
CONV_A_WIDTH = 31
D_SSM = 1024
SSM_HEAD_DIM = 64
SSM_HEADS = 16
SSM_GROUPS = 2
HEADS_PER_GROUP = SSM_HEADS // SSM_GROUPS
GROUP_WIDTH = HEADS_PER_GROUP * SSM_HEAD_DIM
D_STATE = 128
SSM_CONV_WIDTH = 4
CHUNK = 128
D_XBC = D_SSM + 2 * SSM_GROUPS * D_STATE
D_BC = SSM_GROUPS * D_STATE
FFN_RES_WEIGHT = 0.5
NORM_EPS = 1e-5

LANES = 128
SUBLANES = 8
DT_PAD = LANES
A_TAIL = 32
B_TAIL = 8
ROW_TILE = 512
CONV_ROWS = 128
SSD_CHUNKS_PER_STEP = 4
SAMPLE_SSD_BLOCK = 8
SAMPLE_CONV_LANES = 256
CAST_T_BLOCK_ROWS = 256
VMEM_LIMIT = 56 * 1024 * 1024
NEG_BIG = -1e30

ACT_DTYPE = jnp.bfloat16
MXU_DTYPE = jnp.bfloat16


def _dot(a, b):
    return jnp.dot(a.astype(MXU_DTYPE), b.astype(MXU_DTYPE), preferred_element_type=jnp.float32)


def _dot_nt(a, b):
    return lax.dot_general(a.astype(MXU_DTYPE), b.astype(MXU_DTYPE), (((1,), (1,)), ((), ())),
                           preferred_element_type=jnp.float32)


def _split2(v):
    hi = v.astype(MXU_DTYPE)
    lo = (v - hi.astype(jnp.float32)).astype(MXU_DTYPE)
    return hi, lo


def _dot_split_lhs(v, m):
    hi, lo = _split2(v)
    return (jnp.dot(hi, m, preferred_element_type=jnp.float32)
            + jnp.dot(lo, m, preferred_element_type=jnp.float32))


def _dot_split_rhs(m, v):
    hi = v.astype(MXU_DTYPE)
    r1 = v - hi.astype(jnp.float32)
    mid = r1.astype(MXU_DTYPE)
    lo = (r1 - mid.astype(jnp.float32)).astype(MXU_DTYPE)
    return (jnp.dot(m, hi, preferred_element_type=jnp.float32)
            + jnp.dot(m, mid, preferred_element_type=jnp.float32)
            + jnp.dot(m, lo, preferred_element_type=jnp.float32))


def _rms_norm(x, g):
    return x * lax.rsqrt(jnp.mean(x * x, axis=-1, keepdims=True) + NORM_EPS) * g


def _silu(x):
    return x * jax.nn.sigmoid(x)


def _softplus(x):
    return jnp.maximum(x, 0.0) + jnp.log1p(jnp.exp(-jnp.abs(x)))


def _layer_norm_silu(x, g, b):
    mu = jnp.mean(x, axis=-1, keepdims=True)
    xc = x - mu
    var = jnp.mean(xc * xc, axis=-1, keepdims=True)
    return _silu(xc * lax.rsqrt(var + NORM_EPS) * g + b)


def _gate_group_norm(y, z, g):
    y = y * _silu(z)
    parts = []
    for i in range(SSM_GROUPS):
        yg = y[:, i * GROUP_WIDTH:(i + 1) * GROUP_WIDTH]
        parts.append(yg * lax.rsqrt(jnp.mean(yg * yg, axis=-1, keepdims=True) + NORM_EPS))
    return jnp.concatenate(parts, axis=-1) * g


def _causal_conv_tile(ext_ref, w_ref, lanes, row0, n_out, tail, width):
    n_rows = tail + n_out
    x = ext_ref[row0:row0 + n_rows, lanes]
    first = tail - (width - 1)
    acc = jnp.zeros((n_out, LANES), jnp.float32)
    for r in range(SUBLANES):
        taps = [k for k in range(width) if (first + k) % SUBLANES == r]
        if not taps:
            continue
        xr = x if r == 0 else pltpu.roll(x, n_rows - r, 0)
        for k in taps:
            base = first + k - r
            acc = acc + xr[base:base + n_out, :] * w_ref[k:k + 1, lanes]
    return acc


def _const_spec(shape):
    nd = len(shape)
    return pl.BlockSpec(shape, lambda *_: (0,) * nd, pipeline_mode=pl.Buffered(1))


def _full_spec(shape):
    nd = len(shape)
    return pl.BlockSpec(shape, lambda *_: (0,) * nd)


def _params(*semantics):
    return pltpu.CompilerParams(dimension_semantics=semantics, vmem_limit_bytes=VMEM_LIMIT)


def _ffn_kernel(*refs, with_outproj, with_final_norm, n_cast, n_cast_t):
    refs = list(refs)
    x_ref = refs.pop(0)
    if with_outproj:
        ya_ref, ys_ref, wo_ref = refs[:3]
        refs = refs[3:]
    g_ref, wg_ref, wu_ref, wd_ref = refs[:4]
    refs = refs[4:]
    if with_final_norm:
        fg_ref = refs.pop(0)
    n_jobs = n_cast + n_cast_t
    cast_in = refs[:n_jobs]
    o_ref = refs[n_jobs]
    cast_out = refs[n_jobs + 1:]
    for src, dst in zip(cast_in[:n_cast], cast_out[:n_cast]):
        dst[...] = src[...].astype(dst.dtype)
    for src, dst in zip(cast_in[n_cast:], cast_out[n_cast:]):
        for a in range(src.shape[0] // LANES):
            for b in range(src.shape[1] // LANES):
                dst[b * LANES:(b + 1) * LANES, a * LANES:(a + 1) * LANES] = (
                    src[a * LANES:(a + 1) * LANES, b * LANES:(b + 1) * LANES].T.astype(dst.dtype))

    x = x_ref[...]
    if with_outproj:
        x = x + _dot(ya_ref[...], wo_ref[0:D_CONV, :]) + _dot(ys_ref[...], wo_ref[D_CONV:D_CONV + D_SSM, :])
    xn = _rms_norm(x, g_ref[...]).astype(MXU_DTYPE)
    gate = jnp.dot(xn, wg_ref[...], preferred_element_type=jnp.float32)
    up = jnp.dot(xn, wu_ref[...], preferred_element_type=jnp.float32)
    h = (_silu(gate) * up).astype(MXU_DTYPE)
    y = x + FFN_RES_WEIGHT * jnp.dot(h, wd_ref[...], preferred_element_type=jnp.float32)
    if with_final_norm:
        y = _rms_norm(y, fg_ref[...])
    o_ref[...] = y


def _cast_block_rows(n_rows, n_steps):
    tile = 2 * SUBLANES
    n_blocks = max(d for d in range(1, n_steps + 1) if n_rows % d == 0 and (n_rows // d) % tile == 0)
    return n_rows // n_blocks


def _ffn(x, norm_g, wg, wu, wd, ya=None, ys=None, wo=None, final_g=None, cast=(), cast_t=()):
    m = x.shape[0]
    tm = min(m, ROW_TILE)
    n_steps = m // tm
    with_outproj = ya is not None
    with_final_norm = final_g is not None
    row = lambda i: (i, 0)
    args, specs = [x], [pl.BlockSpec((tm, D_MODEL), row)]
    if with_outproj:
        args += [ya, ys, wo]
        specs += [pl.BlockSpec((tm, D_CONV), row), pl.BlockSpec((tm, D_SSM), row), _const_spec(wo.shape)]
    args += [norm_g, wg, wu, wd]
    specs += [_const_spec(norm_g.shape), _const_spec(wg.shape), _const_spec(wu.shape), _const_spec(wd.shape)]
    if with_final_norm:
        args.append(final_g)
        specs.append(_const_spec(final_g.shape))
    cast_specs = []
    for w in cast:
        br = _cast_block_rows(w.shape[0], n_steps)
        last = w.shape[0] // br - 1
        cast_specs.append(pl.BlockSpec((br, w.shape[1]), lambda i, last=last: (jnp.minimum(i, last), 0)))
    out_specs = list(cast_specs)
    out_shapes = [jax.ShapeDtypeStruct(w.shape, MXU_DTYPE) for w in cast]
    for w, n_rows in cast_t:
        br = CAST_T_BLOCK_ROWS
        last = n_rows // br - 1
        assert n_rows % br == 0 and last < n_steps and w.shape[1] % LANES == 0
        cast_specs.append(pl.BlockSpec((br, w.shape[1]), lambda i, last=last: (jnp.minimum(i, last), 0)))
        out_specs.append(pl.BlockSpec((w.shape[1], br), lambda i, last=last: (0, jnp.minimum(i, last))))
        out_shapes.append(jax.ShapeDtypeStruct((w.shape[1], n_rows), MXU_DTYPE))
    out = pl.pallas_call(
        functools.partial(_ffn_kernel, with_outproj=with_outproj, with_final_norm=with_final_norm,
                          n_cast=len(cast), n_cast_t=len(cast_t)),
        grid=(n_steps,),
        in_specs=specs + cast_specs,
        out_specs=[pl.BlockSpec((tm, D_MODEL), row)] + out_specs,
        out_shape=[jax.ShapeDtypeStruct((m, D_MODEL), jnp.float32)] + out_shapes,
        compiler_params=_params("arbitrary"),
        name="ffn_out" if with_outproj else "ffn_in",
    )(*args, *cast, *[w for w, _ in cast_t])
    return out if (cast or cast_t) else out[0]


def _project(x_ref, g_ref, w_ref, wdt_ref, z_ref, xbc_ref, dt_ref):
    xn = _rms_norm(x_ref[...], g_ref[...]).astype(MXU_DTYPE)

    def proj(lo, width):
        return jnp.dot(xn, w_ref[:, lo:lo + width], preferred_element_type=jnp.float32)

    u = proj(0, D_CONV) * jax.nn.sigmoid(proj(D_CONV, D_CONV))
    z_ref[...] = proj(2 * D_CONV, D_SSM).astype(z_ref.dtype)
    xbc_ref[...] = proj(2 * D_CONV + D_SSM, D_XBC).astype(xbc_ref.dtype)
    dt_ref[...] = _dot_nt(xn, wdt_ref[...])
    return u


def _inproj_kernel(x_ref, g_ref, w_ref, wdt_ref, u_ref, z_ref, xbc_ref, dt_ref):
    u_ref[...] = _project(x_ref, g_ref, w_ref, wdt_ref, z_ref, xbc_ref, dt_ref)


def _inproj_conv_kernel(x_ref, g_ref, w_ref, wdt_ref, cwa_ref, cba_ref, lng_ref, lnb_ref,
                        ya_ref, tail_ref, z_ref, xbc_ref, dt_ref, ext_ref, ua_ref, *, tiles_per_seq):
    tm = x_ref.shape[0]

    @pl.when(pl.program_id(0) % tiles_per_seq == 0)
    def _():
        ext_ref[0:A_TAIL, :] = jnp.zeros((A_TAIL, D_CONV), jnp.float32)

    ext_ref[A_TAIL:A_TAIL + tm, :] = _project(x_ref, g_ref, w_ref, wdt_ref, z_ref, xbc_ref, dt_ref)
    for c in range(tm // CONV_ROWS):
        for j in range(D_CONV // LANES):
            lanes = slice(j * LANES, (j + 1) * LANES)
            ua_ref[c * CONV_ROWS:(c + 1) * CONV_ROWS, lanes] = (
                _causal_conv_tile(ext_ref, cwa_ref, lanes, c * CONV_ROWS, CONV_ROWS, A_TAIL, CONV_A_WIDTH)
                + cba_ref[:, lanes])
    tail = ext_ref[tm:tm + A_TAIL, :]
    tail_ref[0] = tail
    ext_ref[0:A_TAIL, :] = tail
    ya_ref[...] = _layer_norm_silu(ua_ref[...], lng_ref[...], lnb_ref[...]).astype(ya_ref.dtype)


def _inproj(x, norm_g, w_main, w_dt):
    m = x.shape[0]
    tm = min(m, ROW_TILE)
    row = lambda i: (i, 0)
    widths = (D_CONV, D_SSM, D_XBC, DT_PAD)
    dtypes = (jnp.float32, ACT_DTYPE, ACT_DTYPE, jnp.float32)
    return pl.pallas_call(
        _inproj_kernel,
        grid=(m // tm,),
        in_specs=[pl.BlockSpec((tm, D_MODEL), row), _const_spec(norm_g.shape), _const_spec(w_main.shape),
                  _const_spec(w_dt.shape)],
        out_specs=[pl.BlockSpec((tm, w), row) for w in widths],
        out_shape=[jax.ShapeDtypeStruct((m, w), d) for w, d in zip(widths, dtypes)],
        compiler_params=_params("arbitrary"),
        name="inproj",
    )(x, norm_g, w_main, w_dt)


def _inproj_conv(x, norm_g, w_main, w_dt, conv, seq_len):
    m = x.shape[0]
    tm = min(seq_len, ROW_TILE)
    tiles_per_seq = seq_len // tm
    f32 = jnp.float32
    row = lambda i: (i, 0)
    return pl.pallas_call(
        functools.partial(_inproj_conv_kernel, tiles_per_seq=tiles_per_seq),
        grid=(m // tm,),
        in_specs=[pl.BlockSpec((tm, D_MODEL), row), _const_spec(norm_g.shape), _const_spec(w_main.shape),
                  _const_spec(w_dt.shape)] + [_const_spec(a.shape) for a in conv],
        out_specs=[pl.BlockSpec((tm, D_CONV), row),
                   pl.BlockSpec((1, A_TAIL, D_CONV), lambda i: (i // tiles_per_seq, 0, 0))]
                  + [pl.BlockSpec((tm, w), row) for w in (D_SSM, D_XBC, DT_PAD)],
        out_shape=[jax.ShapeDtypeStruct((m, D_CONV), ACT_DTYPE),
                   jax.ShapeDtypeStruct((m // seq_len, A_TAIL, D_CONV), f32),
                   jax.ShapeDtypeStruct((m, D_SSM), ACT_DTYPE), jax.ShapeDtypeStruct((m, D_XBC), ACT_DTYPE),
                   jax.ShapeDtypeStruct((m, DT_PAD), f32)],
        scratch_shapes=[pltpu.VMEM((A_TAIL + tm, D_CONV), f32), pltpu.VMEM((tm, D_CONV), f32)],
        compiler_params=_params("arbitrary"),
        name="inproj_conv",
    )(x, norm_g, w_main, w_dt, *conv)


def _ssd_kernel(z_ref, xbc_ref, dt_ref,
                cwb_ref, cbb_ref, dtb_ref, a_ref, dexp_ref, nrm_ref, e_ref,
                ys_ref, ssm_ref,
                extb_ref, ht_ref, xc_ref, y_ref):
    c = pl.program_id(1)
    f32 = jnp.float32
    n_rows = SSD_CHUNKS_PER_STEP * CHUNK

    @pl.when(c == 0)
    def _():
        extb_ref[0:B_TAIL, :] = jnp.zeros((B_TAIL, D_XBC), f32)
        ht_ref[...] = jnp.zeros_like(ht_ref)

    extb_ref[B_TAIL:B_TAIL + n_rows, :] = xbc_ref[0].astype(f32)
    for q in range(SSD_CHUNKS_PER_STEP):
        for j in range(D_XBC // LANES):
            lanes = slice(j * LANES, (j + 1) * LANES)
            xc_ref[q * CHUNK:(q + 1) * CHUNK, lanes] = _silu(
                _causal_conv_tile(extb_ref, cwb_ref, lanes, q * CHUNK, CHUNK, B_TAIL, SSM_CONV_WIDTH)
                + cbb_ref[:, lanes])
    extb_ref[0:B_TAIL, :] = extb_ref[n_rows:n_rows + B_TAIL, :]

    rows = lax.broadcasted_iota(jnp.int32, (CHUNK, CHUNK), 0)
    cols = lax.broadcasted_iota(jnp.int32, (CHUNK, CHUNK), 1)
    causal = rows >= cols
    tri = causal.astype(MXU_DTYPE)
    lane = lax.broadcasted_iota(jnp.int32, (CHUNK, LANES), 1)
    e_mat = e_ref[...]
    gw = GROUP_WIDTH
    for q in range(SSD_CHUNKS_PER_STEP):
        rs = slice(q * CHUNK, (q + 1) * CHUNK)
        dt = _softplus(dt_ref[0, rs, :] + dtb_ref[...])
        a = dt * a_ref[...]
        a_cs = _dot_split_rhs(tri, a)
        a_cs_t = a_cs.T
        dt_t = dt.T
        ea = jnp.exp(a_cs)
        w_end = jnp.exp(a_cs[CHUNK - 1:CHUNK, :] - a_cs) * dt
        w_exp = _dot_split_lhs(w_end, e_mat)
        ea_exp = _dot_split_lhs(ea, e_mat)
        xs = xc_ref[rs, 0:D_SSM]
        xs_b = xs.astype(MXU_DTYPE)
        xw_b = (xs * w_exp).astype(MXU_DTYPE)
        for g in range(SSM_GROUPS):
            b_g = xc_ref[rs, D_SSM + g * D_STATE:D_SSM + (g + 1) * D_STATE]
            c_g = xc_ref[rs, D_SSM + D_BC + g * D_STATE:D_SSM + D_BC + (g + 1) * D_STATE]
            c_gb = c_g.astype(MXU_DTYPE)
            cb = _dot_nt(c_gb, b_g)
            ht_g = ht_ref[:, g * gw:(g + 1) * gw]
            y_off = _dot(c_gb, ht_g) * ea_exp[:, g * gw:(g + 1) * gw]
            for hp in range(HEADS_PER_GROUP // 2):
                res = []
                for e in range(2):
                    h = g * HEADS_PER_GROUP + 2 * hp + e
                    seg = a_cs[:, h:h + 1] - a_cs_t[h:h + 1, :]
                    l_mat = jnp.exp(jnp.where(causal, seg, NEG_BIG))
                    gmat = (cb * l_mat * dt_t[h:h + 1, :]).astype(MXU_DTYPE)
                    lo = g * gw + hp * LANES
                    res.append(jnp.dot(gmat, xs_b[:, lo:lo + LANES], preferred_element_type=f32))
                y_pair = jnp.where(lane < SSM_HEAD_DIM, res[0], res[1])
                y_ref[rs, lo:lo + LANES] = y_pair + y_off[:, hp * LANES:(hp + 1) * LANES]
            s_loc = _dot(b_g.T, xw_b[:, g * gw:(g + 1) * gw])
            ht_ref[:, g * gw:(g + 1) * gw] = ht_g * ea_exp[CHUNK - 1:CHUNK, g * gw:(g + 1) * gw] + s_loc

        y = y_ref[rs, :] + dexp_ref[...] * xs
        ys_ref[0, rs, :] = _gate_group_norm(y, z_ref[0, rs, :].astype(f32), nrm_ref[...]).astype(ys_ref.dtype)

    @pl.when(c == pl.num_programs(1) - 1)
    def _():
        for j in range(D_SSM // LANES):
            ssm_ref[0, j * LANES:(j + 1) * LANES, :] = ht_ref[:, j * LANES:(j + 1) * LANES].T


def _ssd(z, xbc, dt_raw, p):
    b, l, _ = z.shape
    n_rows = SSD_CHUNKS_PER_STEP * CHUNK
    tok = lambda i, c: (i, c, 0)
    consts = [p["cwb"], p["cbb"], p["dtb"], p["a"], p["dexp"], p["nrm"], p["e"]]
    f32 = jnp.float32
    return pl.pallas_call(
        _ssd_kernel,
        grid=(b, l // n_rows),
        in_specs=[pl.BlockSpec((1, n_rows, D_SSM), tok), pl.BlockSpec((1, n_rows, D_XBC), tok),
                  pl.BlockSpec((1, n_rows, DT_PAD), tok)] + [_const_spec(a.shape) for a in consts],
        out_specs=[pl.BlockSpec((1, n_rows, D_SSM), tok),
                   pl.BlockSpec((1, D_SSM, D_STATE), lambda i, c: (i, 0, 0))],
        out_shape=[jax.ShapeDtypeStruct((b, l, D_SSM), ACT_DTYPE),
                   jax.ShapeDtypeStruct((b, D_SSM, D_STATE), f32)],
        scratch_shapes=[pltpu.VMEM((B_TAIL + n_rows, D_XBC), f32),
                        pltpu.VMEM((D_STATE, D_SSM), f32),
                        pltpu.VMEM((n_rows, D_XBC), f32),
                        pltpu.VMEM((n_rows, D_SSM), f32)],
        compiler_params=_params("arbitrary", "arbitrary"),
        name="ssd",
    )(z, xbc, dt_raw, *consts)


def _sample_conv_a_kernel(st_ref, u_ref, w_ref, cba_ref, new_ref, ua_ref):
    hist = CONV_A_WIDTH - 1
    for j in range(u_ref.shape[1] // LANES):
        lanes = slice(j * LANES, (j + 1) * LANES)
        u = u_ref[:, lanes]
        acc = u * w_ref[hist:hist + 1, lanes] + cba_ref[:, lanes]
        for k in range(hist):
            row = st_ref[k, :, lanes]
            acc = acc + row * w_ref[k:k + 1, lanes]
            if k >= 1:
                new_ref[k - 1, :, lanes] = row
        new_ref[hist - 1, :, lanes] = u
        ua_ref[:, lanes] = acc


def _sample_conv_a(state_k, u, p):
    hist, s, _ = state_k.shape
    wb = SAMPLE_CONV_LANES
    f32 = jnp.float32
    blk3 = pl.BlockSpec((hist, s, wb), lambda j: (0, 0, j))
    blk2 = pl.BlockSpec((s, wb), lambda j: (0, j))
    return pl.pallas_call(
        _sample_conv_a_kernel,
        grid=(D_CONV // wb,),
        in_specs=[blk3, blk2, pl.BlockSpec((A_TAIL, wb), lambda j: (0, j)), pl.BlockSpec((1, wb), lambda j: (0, j))],
        out_specs=[blk3, blk2],
        out_shape=[jax.ShapeDtypeStruct((hist, s, D_CONV), f32), jax.ShapeDtypeStruct((s, D_CONV), f32)],
        compiler_params=_params("arbitrary"),
        name="sample_conv_a",
    )(state_k, u, p["cwa"], p["cba"])


def _sample_conv_b_kernel(st_ref, xbc_ref, dt_ref, cwb_ref, cbb_ref, dtb_ref, a_ref, e_ref,
                          new_ref, xs_ref, xt_ref, dat_ref, b_ref, ct_ref):
    hist = SSM_CONV_WIDTH - 1
    xbc = xbc_ref[...].astype(jnp.float32)
    acc = xbc * cwb_ref[hist:hist + 1, :] + cbb_ref[...]
    for k in range(hist):
        acc = acc + st_ref[k] * cwb_ref[k:k + 1, :]
    for k in range(hist - 1):
        new_ref[k] = st_ref[k + 1]
    new_ref[hist - 1] = xbc
    xc = _silu(acc)
    xs = xc[:, 0:D_SSM]
    dt = _softplus(dt_ref[...] + dtb_ref[...])
    da = jnp.exp(dt * a_ref[...])
    xs_ref[...] = xs
    x_dt = xs * _dot_split_lhs(dt, e_ref[...])
    for j in range(D_SSM // LANES):
        xt_ref[j * LANES:(j + 1) * LANES, :] = x_dt[:, j * LANES:(j + 1) * LANES].T.astype(xt_ref.dtype)
    dat_ref[...] = da.T[0:SSM_HEADS, :]
    b_ref[...] = xc[:, D_SSM:D_SSM + D_BC]
    for g in range(SSM_GROUPS):
        lo = D_SSM + D_BC + g * D_STATE
        ct_ref[g * D_STATE:(g + 1) * D_STATE, :] = xc[:, lo:lo + D_STATE].T.astype(ct_ref.dtype)


def _sample_conv_b(state_k, xbc, dt_raw, p):
    hist, s, _ = state_k.shape
    f32 = jnp.float32
    args = [state_k, xbc, dt_raw, p["cwb"], p["cbb"], p["dtb"], p["a"], p["e"]]
    shapes = [((hist, s, D_XBC), f32), ((s, D_SSM), f32), ((D_SSM, s), MXU_DTYPE), ((SSM_HEADS, s), f32),
              ((s, D_BC), f32), ((D_BC, s), MXU_DTYPE)]
    return pl.pallas_call(
        _sample_conv_b_kernel,
        grid=(1,),
        in_specs=[_const_spec(a.shape) for a in args],
        out_specs=[_full_spec(sh) for sh, _ in shapes],
        out_shape=[jax.ShapeDtypeStruct(sh, d) for sh, d in shapes],
        compiler_params=_params("arbitrary"),
        name="sample_conv_b",
    )(*args)


def _sample_ssd_kernel(dat_ref, st_ref, xt_ref, b_ref, ct_ref, new_ref, yt_ref):
    i = pl.program_id(0)
    bs = st_ref.shape[0]
    n_seq = yt_ref.shape[1]
    f32 = jnp.float32

    @pl.when(i == 0)
    def _():
        yt_ref[...] = jnp.zeros_like(yt_ref)

    row_id = lax.broadcasted_iota(jnp.int32, (n_seq, D_STATE), 0)
    col_id = lax.broadcasted_iota(jnp.int32, (GROUP_WIDTH, n_seq), 1)

    def body(sl, carry):
        s = i * bs + sl
        for g in range(SSM_GROUPS):
            rows = slice(g * GROUP_WIDTH, (g + 1) * GROUP_WIDTH)
            b_sel = jnp.where(row_id == s, b_ref[:, g * D_STATE:(g + 1) * D_STATE], 0.0).astype(MXU_DTYPE)
            outer = jnp.dot(xt_ref[rows, :], b_sel, preferred_element_type=f32)
            h_new = []
            for hh in range(HEADS_PER_GROUP):
                h = g * HEADS_PER_GROUP + hh
                hn = st_ref[sl, h] * dat_ref[h, s] + outer[hh * SSM_HEAD_DIM:(hh + 1) * SSM_HEAD_DIM, :]
                new_ref[sl, h] = hn
                h_new.append(hn.astype(MXU_DTYPE))
            y_all = jnp.dot(jnp.concatenate(h_new, axis=0), ct_ref[g * D_STATE:(g + 1) * D_STATE, :],
                            preferred_element_type=f32)
            yt_ref[rows, :] = jnp.where(col_id == s, y_all, yt_ref[rows, :])
        return carry

    lax.fori_loop(0, bs, body, 0, unroll=2)


def _sample_ssd(state, xt, dat, b, ct):
    s = state.shape[0]
    bs = SAMPLE_SSD_BLOCK
    blk = pl.BlockSpec((bs, SSM_HEADS, SSM_HEAD_DIM, D_STATE), lambda i: (i, 0, 0, 0))
    return pl.pallas_call(
        _sample_ssd_kernel,
        grid=(s // bs,),
        in_specs=[pl.BlockSpec(memory_space=pltpu.SMEM), blk,
                  _const_spec(xt.shape), _const_spec(b.shape), _const_spec(ct.shape)],
        out_specs=[blk, _full_spec((D_SSM, s))],
        out_shape=[jax.ShapeDtypeStruct(state.shape, jnp.float32), jax.ShapeDtypeStruct((D_SSM, s), jnp.float32)],
        compiler_params=_params("arbitrary"),
        name="sample_ssd",
    )(dat, state, xt, b, ct)


def _sample_gate_kernel(ua_ref, lng_ref, lnb_ref, yt_ref, xs_ref, z_ref, dexp_ref, nrm_ref, ya_ref, ys_ref):
    ya_ref[...] = _layer_norm_silu(ua_ref[...], lng_ref[...], lnb_ref[...]).astype(ya_ref.dtype)
    y = jnp.concatenate([yt_ref[j * LANES:(j + 1) * LANES, :].T for j in range(D_SSM // LANES)], axis=-1)
    y = y + dexp_ref[...] * xs_ref[...]
    ys_ref[...] = _gate_group_norm(y, z_ref[...].astype(jnp.float32), nrm_ref[...]).astype(ys_ref.dtype)


def _sample_gate(ua, yt, xs, z, p):
    s = xs.shape[0]
    args = [ua, p["lng"], p["lnb"], yt, xs, z, p["dexp"], p["nrm"]]
    return pl.pallas_call(
        _sample_gate_kernel,
        grid=(1,),
        in_specs=[_const_spec(a.shape) for a in args],
        out_specs=[_full_spec((s, D_CONV)), _full_spec((s, D_SSM))],
        out_shape=[jax.ShapeDtypeStruct((s, D_CONV), ACT_DTYPE), jax.ShapeDtypeStruct((s, D_SSM), ACT_DTYPE)],
        compiler_params=_params("arbitrary"),
        name="sample_gate",
    )(*args)


def _prep_params(i, conv_dw_w, conv_dw_b, conv_ln_g, conv_ln_b, ssm_conv_w, ssm_conv_b, ssm_dt_bias, ssm_a_log,
                 ssm_d, ssm_norm):
    f32 = jnp.float32
    row = lambda v: v.reshape(1, -1).astype(f32)
    pad_heads = lambda v: jnp.pad(v.astype(f32), (0, DT_PAD - SSM_HEADS)).reshape(1, DT_PAD)
    head_of_channel = jnp.arange(D_SSM, dtype=jnp.int32) // SSM_HEAD_DIM
    expand = (jnp.arange(DT_PAD, dtype=jnp.int32)[:, None] == head_of_channel[None, :]).astype(MXU_DTYPE)
    return {
        "cwa": jnp.pad(conv_dw_w[i].astype(f32), ((0, A_TAIL - CONV_A_WIDTH), (0, 0))),
        "cba": row(conv_dw_b[i]), "lng": row(conv_ln_g[i]), "lnb": row(conv_ln_b[i]),
        "cwb": jnp.pad(ssm_conv_w[i].astype(f32), ((0, B_TAIL - SSM_CONV_WIDTH), (0, 0))),
        "cbb": row(ssm_conv_b[i]),
        "dtb": pad_heads(ssm_dt_bias[i]),
        "a": pad_heads(-jnp.exp(ssm_a_log[i].astype(f32))),
        "dexp": jnp.repeat(ssm_d[i].astype(f32), SSM_HEAD_DIM).reshape(1, D_SSM),
        "nrm": row(ssm_norm[i]),
        "e": expand,
    }


def kernel(x_prompt, x_sample, state_conv_a, state_conv_b, state_ssm, ffn1_norm, ffn1_w_gate, ffn1_w_up, ffn1_w_down, mix_norm, w_in, conv_dw_w, conv_dw_b, conv_ln_g, conv_ln_b, ssm_conv_w, ssm_conv_b, ssm_dt_bias, ssm_a_log, ssm_d, ssm_norm, w_out, ffn2_norm, ffn2_w_gate, ffn2_w_up, ffn2_w_down, final_norm):
    depth = ffn1_norm.shape[0]
    bp, lp, _ = x_prompt.shape
    bs = x_sample.shape[0]
    f32 = jnp.float32
    row = lambda v: v.reshape(1, -1).astype(f32)
    w16 = lambda w: w.astype(MXU_DTYPE)

    yp = x_prompt.reshape(bp * lp, D_MODEL)
    ysm = x_sample.reshape(bs, D_MODEL)
    outs = [[] for _ in range(6)]
    for i in range(depth):
        p = _prep_params(i, conv_dw_w, conv_dw_b, conv_ln_g, conv_ln_b, ssm_conv_w, ssm_conv_b, ssm_dt_bias,
                         ssm_a_log, ssm_d, ssm_norm)
        w_in_t = jnp.transpose(w_in[i])
        n_main = 2 * D_CONV + D_SSM + D_XBC
        w_dt = jnp.pad(w_in_t[n_main:], ((0, DT_PAD - SSM_HEADS), (0, 0)))
        f1 = (row(ffn1_norm[i]), w16(ffn1_w_gate[i]), w16(ffn1_w_up[i]), w16(ffn1_w_down[i]))
        fin = row(final_norm) if i == depth - 1 else None

        x1, w2g, w2u, w2d, wo, w_main = _ffn(
            yp, *f1, cast=(ffn2_w_gate[i], ffn2_w_up[i], ffn2_w_down[i], w_out[i]), cast_t=((w_in_t, n_main),))
        f2 = (row(ffn2_norm[i]), w2g, w2u, w2d)
        ya, a_tail, z, xbc, dt_raw = _inproj_conv(x1, row(mix_norm[i]), w_main, w_dt,
                                                  (p["cwa"], p["cba"], p["lng"], p["lnb"]), lp)
        xbc3 = xbc.reshape(bp, lp, D_XBC)
        ys, ssm_t = _ssd(z.reshape(bp, lp, D_SSM), xbc3, dt_raw.reshape(bp, lp, DT_PAD), p)
        yp = _ffn(x1, *f2, ya=ya, ys=ys.reshape(bp * lp, D_SSM), wo=wo, final_g=fin)
        outs[0].append(a_tail[:, A_TAIL - (CONV_A_WIDTH - 1):, :].astype(x_prompt.dtype))
        outs[1].append(xbc3[:, lp - (SSM_CONV_WIDTH - 1):, :].astype(x_prompt.dtype))
        outs[2].append(ssm_t.reshape(bp, SSM_HEADS, SSM_HEAD_DIM, D_STATE).astype(state_ssm.dtype))

        s1 = _ffn(ysm, *f1)
        su, sz, sxbc, sdt = _inproj(s1, row(mix_norm[i]), w_main, w_dt)
        tap_major = lambda st: jnp.transpose(st.astype(f32), (1, 0, 2))
        new_a, sua = _sample_conv_a(tap_major(state_conv_a[i]), su, p)
        new_b, xs, xt, dat, b_mat, ct = _sample_conv_b(tap_major(state_conv_b[i]), sxbc, sdt, p)
        new_s, yt = _sample_ssd(state_ssm[i].astype(f32), xt, dat, b_mat, ct)
        sya, sys_ = _sample_gate(sua, yt, xs, sz, p)
        ysm = _ffn(s1, *f2, ya=sya, ys=sys_, wo=wo, final_g=fin)
        outs[3].append(jnp.transpose(new_a, (1, 0, 2)).astype(x_sample.dtype))
        outs[4].append(jnp.transpose(new_b, (1, 0, 2)).astype(x_sample.dtype))
        outs[5].append(new_s.astype(state_ssm.dtype))

    return (yp.reshape(bp, lp, D_MODEL), ysm.reshape(bs, 1, D_MODEL)) + tuple(jnp.stack(o) for o in outs)
```

```python
import functools

import jax
import jax.numpy as jnp
from jax import lax
from jax.experimental import pallas as pl
from jax.experimental.pallas import tpu as pltpu

D_MODEL = 1024
D_FF = 2816
D_CONV = 1024
CONV_A_WIDTH = 31
D_SSM = 1024
SSM_HEAD_DIM = 64
SSM_HEADS = 16
SSM_GROUPS = 2
HEADS_PER_GROUP = SSM_HEADS // SSM_GROUPS
GROUP_WIDTH = HEADS_PER_GROUP * SSM_HEAD_DIM
D_STATE = 128
SSM_CONV_WIDTH = 4
CHUNK = 128
D_XBC = D_SSM + 2 * SSM_GROUPS * D_STATE
D_BC = SSM_GROUPS * D_STATE
FFN_RES_WEIGHT = 0.5
NORM_EPS = 1e-5

LANES = 128
SUBLANES = 8
DT_PAD = LANES
A_TAIL = 32
B_TAIL = 8
ROW_TILE = 512
CONV_ROWS = 128
SSD_CHUNKS_PER_STEP = 2
SAMPLE_SSD_BLOCK = 16
SAMPLE_CONV_LANES = 256
CAST_T_BLOCK_ROWS = 256
VMEM_LIMIT = 56 * 1024 * 1024
NEG_BIG = -1e30

ACT_DTYPE = jnp.bfloat16
MXU_DTYPE = jnp.bfloat16


def _dot(a, b):
    return jnp.dot(a.astype(MXU_DTYPE), b.astype(MXU_DTYPE), preferred_element_type=jnp.float32)


def _dot_nt(a, b):
    return lax.dot_general(a.astype(MXU_DTYPE), b.astype(MXU_DTYPE), (((1,), (1,)), ((), ())),
                           preferred_element_type=jnp.float32)


def _split2(v):
    hi = v.astype(MXU_DTYPE)
    lo = (v - hi.astype(jnp.float32)).astype(MXU_DTYPE)
    return hi, lo


def _dot_split_lhs(v, m):
    hi, lo = _split2(v)
    return (jnp.dot(hi, m, preferred_element_type=jnp.float32)
            + jnp.dot(lo, m, preferred_element_type=jnp.float32))


def _dot_split_rhs(m, v):
    hi = v.astype(MXU_DTYPE)
    r1 = v - hi.astype(jnp.float32)
    mid = r1.astype(MXU_DTYPE)
    lo = (r1 - mid.astype(jnp.float32)).astype(MXU_DTYPE)
    return (jnp.dot(m, hi, preferred_element_type=jnp.float32)
            + jnp.dot(m, mid, preferred_element_type=jnp.float32)
            + jnp.dot(m, lo, preferred_element_type=jnp.float32))


def _rms_norm(x, g):
    return x * lax.rsqrt(jnp.mean(x * x, axis=-1, keepdims=True) + NORM_EPS) * g


def _silu(x):
    return x * jax.nn.sigmoid(x)


def _softplus(x):
    return jnp.maximum(x, 0.0) + jnp.log1p(jnp.exp(-jnp.abs(x)))


def _layer_norm_silu(x, g, b):
    mu = jnp.mean(x, axis=-1, keepdims=True)
    xc = x - mu
    var = jnp.mean(xc * xc, axis=-1, keepdims=True)
    return _silu(xc * lax.rsqrt(var + NORM_EPS) * g + b)


def _gate_group_norm(y, z, g):
    y = y * _silu(z)
    parts = []
    for i in range(SSM_GROUPS):
        yg = y[:, i * GROUP_WIDTH:(i + 1) * GROUP_WIDTH]
        parts.append(yg * lax.rsqrt(jnp.mean(yg * yg, axis=-1, keepdims=True) + NORM_EPS))
    return jnp.concatenate(parts, axis=-1) * g


def _causal_conv_tile(ext_ref, w_ref, lanes, row0, n_out, tail, width):
    n_rows = tail + n_out
    x = ext_ref[row0:row0 + n_rows, lanes]
    first = tail - (width - 1)
    acc = jnp.zeros((n_out, LANES), jnp.float32)
    for r in range(SUBLANES):
        taps = [k for k in range(width) if (first + k) % SUBLANES == r]
        if not taps:
            continue
        xr = x if r == 0 else pltpu.roll(x, n_rows - r, 0)
        for k in taps:
            base = first + k - r
            acc = acc + xr[base:base + n_out, :] * w_ref[k:k + 1, lanes]
    return acc


def _const_spec(shape):
    nd = len(shape)
    return pl.BlockSpec(shape, lambda *_: (0,) * nd, pipeline_mode=pl.Buffered(1))


def _full_spec(shape):
    nd = len(shape)
    return pl.BlockSpec(shape, lambda *_: (0,) * nd)


def _params(*semantics):
    return pltpu.CompilerParams(dimension_semantics=semantics, vmem_limit_bytes=VMEM_LIMIT)


def _ffn_kernel(*refs, with_outproj, with_final_norm, n_cast, n_cast_t):
    refs = list(refs)
    x_ref = refs.pop(0)
    if with_outproj:
        ya_ref, ys_ref, wo_ref = refs[:3]
        refs = refs[3:]
    g_ref, wg_ref, wu_ref, wd_ref = refs[:4]
    refs = refs[4:]
    if with_final_norm:
        fg_ref = refs.pop(0)
    n_jobs = n_cast + n_cast_t
    cast_in = refs[:n_jobs]
    o_ref = refs[n_jobs]
    cast_out = refs[n_jobs + 1:]
    for src, dst in zip(cast_in[:n_cast], cast_out[:n_cast]):
        dst[...] = src[...].astype(dst.dtype)
    for src, dst in zip(cast_in[n_cast:], cast_out[n_cast:]):
        for a in range(src.shape[0] // LANES):
            for b in range(src.shape[1] // LANES):
                dst[b * LANES:(b + 1) * LANES, a * LANES:(a + 1) * LANES] = (
                    src[a * LANES:(a + 1) * LANES, b * LANES:(b + 1) * LANES].T.astype(dst.dtype))

    x = x_ref[...]
    if with_outproj:
        x = x + _dot(ya_ref[...], wo_ref[0:D_CONV, :]) + _dot(ys_ref[...], wo_ref[D_CONV:D_CONV + D_SSM, :])
    xn = _rms_norm(x, g_ref[...]).astype(MXU_DTYPE)
    gate = jnp.dot(xn, wg_ref[...], preferred_element_type=jnp.float32)
    up = jnp.dot(xn, wu_ref[...], preferred_element_type=jnp.float32)
    h = (_silu(gate) * up).astype(MXU_DTYPE)
    y = x + FFN_RES_WEIGHT * jnp.dot(h, wd_ref[...], preferred_element_type=jnp.float32)
    if with_final_norm:
        y = _rms_norm(y, fg_ref[...])
    o_ref[...] = y


def _cast_block_rows(n_rows, n_steps):
    tile = 2 * SUBLANES
    n_blocks = max(d for d in range(1, n_steps + 1) if n_rows % d == 0 and (n_rows // d) % tile == 0)
    return n_rows // n_blocks


def _ffn(x, norm_g, wg, wu, wd, ya=None, ys=None, wo=None, final_g=None, cast=(), cast_t=()):
    m = x.shape[0]
    tm = min(m, ROW_TILE)
    n_steps = m // tm
    with_outproj = ya is not None
    with_final_norm = final_g is not None
    row = lambda i: (i, 0)
    args, specs = [x], [pl.BlockSpec((tm, D_MODEL), row)]
    if with_outproj:
        args += [ya, ys, wo]
        specs += [pl.BlockSpec((tm, D_CONV), row), pl.BlockSpec((tm, D_SSM), row), _const_spec(wo.shape)]
    args += [norm_g, wg, wu, wd]
    specs += [_const_spec(norm_g.shape), _const_spec(wg.shape), _const_spec(wu.shape), _const_spec(wd.shape)]
    if with_final_norm:
        args.append(final_g)
        specs.append(_const_spec(final_g.shape))
    cast_specs = []
    for w in cast:
        br = _cast_block_rows(w.shape[0], n_steps)
        last = w.shape[0] // br - 1
        cast_specs.append(pl.BlockSpec((br, w.shape[1]), lambda i, last=last: (jnp.minimum(i, last), 0)))
    out_specs = list(cast_specs)
    out_shapes = [jax.ShapeDtypeStruct(w.shape, MXU_DTYPE) for w in cast]
    for w, n_rows in cast_t:
        br = CAST_T_BLOCK_ROWS
        last = n_rows // br - 1
        assert n_rows % br == 0 and last < n_steps and w.shape[1] % LANES == 0
        cast_specs.append(pl.BlockSpec((br, w.shape[1]), lambda i, last=last: (jnp.minimum(i, last), 0)))
        out_specs.append(pl.BlockSpec((w.shape[1], br), lambda i, last=last: (0, jnp.minimum(i, last))))
        out_shapes.append(jax.ShapeDtypeStruct((w.shape[1], n_rows), MXU_DTYPE))
    out = pl.pallas_call(
        functools.partial(_ffn_kernel, with_outproj=with_outproj, with_final_norm=with_final_norm,
                          n_cast=len(cast), n_cast_t=len(cast_t)),
        grid=(n_steps,),
        in_specs=specs + cast_specs,
        out_specs=[pl.BlockSpec((tm, D_MODEL), row)] + out_specs,
        out_shape=[jax.ShapeDtypeStruct((m, D_MODEL), jnp.float32)] + out_shapes,
        compiler_params=_params("arbitrary"),
        name="ffn_out" if with_outproj else "ffn_in",
    )(*args, *cast, *[w for w, _ in cast_t])
    return out if (cast or cast_t) else out[0]


def _project(x_ref, g_ref, w_ref, wdt_ref, z_ref, xbc_ref, dt_ref):
    xn = _rms_norm(x_ref[...], g_ref[...]).astype(MXU_DTYPE)

    def proj(lo, width):
        return jnp.dot(xn, w_ref[:, lo:lo + width], preferred_element_type=jnp.float32)

    u = proj(0, D_CONV) * jax.nn.sigmoid(proj(D_CONV, D_CONV))
    z_ref[...] = proj(2 * D_CONV, D_SSM).astype(z_ref.dtype)
    xbc_ref[...] = proj(2 * D_CONV + D_SSM, D_XBC).astype(xbc_ref.dtype)
    dt_ref[...] = _dot_nt(xn, wdt_ref[...])
    return u


def _inproj_kernel(x_ref, g_ref, w_ref, wdt_ref, u_ref, z_ref, xbc_ref, dt_ref):
    u_ref[...] = _project(x_ref, g_ref, w_ref, wdt_ref, z_ref, xbc_ref, dt_ref)


def _inproj_conv_kernel(x_ref, g_ref, w_ref, wdt_ref, cwa_ref, cba_ref, lng_ref, lnb_ref,
                        ya_ref, tail_ref, z_ref, xbc_ref, dt_ref, ext_ref, ua_ref, *, tiles_per_seq):
    tm = x_ref.shape[0]

    @pl.when(pl.program_id(0) % tiles_per_seq == 0)
    def _():
        ext_ref[0:A_TAIL, :] = jnp.zeros((A_TAIL, D_CONV), jnp.float32)

    ext_ref[A_TAIL:A_TAIL + tm, :] = _project(x_ref, g_ref, w_ref, wdt_ref, z_ref, xbc_ref, dt_ref)
    for c in range(tm // CONV_ROWS):
        for j in range(D_CONV // LANES):
            lanes = slice(j * LANES, (j + 1) * LANES)
            ua_ref[c * CONV_ROWS:(c + 1) * CONV_ROWS, lanes] = (
                _causal_conv_tile(ext_ref, cwa_ref, lanes, c * CONV_ROWS, CONV_ROWS, A_TAIL, CONV_A_WIDTH)
                + cba_ref[:, lanes])
    tail = ext_ref[tm:tm + A_TAIL, :]
    tail_ref[0] = tail
    ext_ref[0:A_TAIL, :] = tail
    ya_ref[...] = _layer_norm_silu(ua_ref[...], lng_ref[...], lnb_ref[...]).astype(ya_ref.dtype)


def _inproj(x, norm_g, w_main, w_dt):
    m = x.shape[0]
    tm = min(m, ROW_TILE)
    row = lambda i: (i, 0)
    widths = (D_CONV, D_SSM, D_XBC, DT_PAD)
    dtypes = (jnp.float32, ACT_DTYPE, ACT_DTYPE, jnp.float32)
    return pl.pallas_call(
        _inproj_kernel,
        grid=(m // tm,),
        in_specs=[pl.BlockSpec((tm, D_MODEL), row), _const_spec(norm_g.shape), _const_spec(w_main.shape),
                  _const_spec(w_dt.shape)],
        out_specs=[pl.BlockSpec((tm, w), row) for w in widths],
        out_shape=[jax.ShapeDtypeStruct((m, w), d) for w, d in zip(widths, dtypes)],
        compiler_params=_params("arbitrary"),
        name="inproj",
    )(x, norm_g, w_main, w_dt)


def _inproj_conv(x, norm_g, w_main, w_dt, conv, seq_len):
    m = x.shape[0]
    tm = min(seq_len, ROW_TILE)
    tiles_per_seq = seq_len // tm
    f32 = jnp.float32
    row = lambda i: (i, 0)
    return pl.pallas_call(
        functools.partial(_inproj_conv_kernel, tiles_per_seq=tiles_per_seq),
        grid=(m // tm,),
        in_specs=[pl.BlockSpec((tm, D_MODEL), row), _const_spec(norm_g.shape), _const_spec(w_main.shape),
                  _const_spec(w_dt.shape)] + [_const_spec(a.shape) for a in conv],
        out_specs=[pl.BlockSpec((tm, D_CONV), row),
                   pl.BlockSpec((1, A_TAIL, D_CONV), lambda i: (i // tiles_per_seq, 0, 0))]
                  + [pl.BlockSpec((tm, w), row) for w in (D_SSM, D_XBC, DT_PAD)],
        out_shape=[jax.ShapeDtypeStruct((m, D_CONV), ACT_DTYPE),
                   jax.ShapeDtypeStruct((m // seq_len, A_TAIL, D_CONV), f32),
                   jax.ShapeDtypeStruct((m, D_SSM), ACT_DTYPE), jax.ShapeDtypeStruct((m, D_XBC), ACT_DTYPE),
                   jax.ShapeDtypeStruct((m, DT_PAD), f32)],
        scratch_shapes=[pltpu.VMEM((A_TAIL + tm, D_CONV), f32), pltpu.VMEM((tm, D_CONV), f32)],
        compiler_params=_params("arbitrary"),
        name="inproj_conv",
    )(x, norm_g, w_main, w_dt, *conv)


def _ssd_kernel(z_ref, xbc_ref, dt_ref,
                cwb_ref, cbb_ref, dtb_ref, a_ref, dexp_ref, nrm_ref, e_ref,
                ys_ref, ssm_ref,
                extb_ref, ht_ref, xc_ref, y_ref):
    c = pl.program_id(1)
    f32 = jnp.float32
    n_rows = SSD_CHUNKS_PER_STEP * CHUNK

    @pl.when(c == 0)
    def _():
        extb_ref[0:B_TAIL, :] = jnp.zeros((B_TAIL, D_XBC), f32)
        ht_ref[...] = jnp.zeros_like(ht_ref)

    extb_ref[B_TAIL:B_TAIL + n_rows, :] = xbc_ref[0].astype(f32)
    for q in range(SSD_CHUNKS_PER_STEP):
        for j in range(D_XBC // LANES):
            lanes = slice(j * LANES, (j + 1) * LANES)
            xc_ref[q * CHUNK:(q + 1) * CHUNK, lanes] = _silu(
                _causal_conv_tile(extb_ref, cwb_ref, lanes, q * CHUNK, CHUNK, B_TAIL, SSM_CONV_WIDTH)
                + cbb_ref[:, lanes])
    extb_ref[0:B_TAIL, :] = extb_ref[n_rows:n_rows + B_TAIL, :]

    rows = lax.broadcasted_iota(jnp.int32, (CHUNK, CHUNK), 0)
    cols = lax.broadcasted_iota(jnp.int32, (CHUNK, CHUNK), 1)
    causal = rows >= cols
    tri = causal.astype(MXU_DTYPE)
    lane = lax.broadcasted_iota(jnp.int32, (CHUNK, LANES), 1)
    e_mat = e_ref[...]
    gw = GROUP_WIDTH
    for q in range(SSD_CHUNKS_PER_STEP):
        rs = slice(q * CHUNK, (q + 1) * CHUNK)
        dt = _softplus(dt_ref[0, rs, :] + dtb_ref[...])
        a = dt * a_ref[...]
        a_cs = _dot_split_rhs(tri, a)
        a_cs_t = a_cs.T
        dt_t = dt.T
        ea = jnp.exp(a_cs)
        w_end = jnp.exp(a_cs[CHUNK - 1:CHUNK, :] - a_cs) * dt
        w_exp = _dot_split_lhs(w_end, e_mat)
        ea_exp = _dot_split_lhs(ea, e_mat)
        xs = xc_ref[rs, 0:D_SSM]
        xs_b = xs.astype(MXU_DTYPE)
        xw_b = (xs * w_exp).astype(MXU_DTYPE)
        for g in range(SSM_GROUPS):
            b_g = xc_ref[rs, D_SSM + g * D_STATE:D_SSM + (g + 1) * D_STATE]
            c_g = xc_ref[rs, D_SSM + D_BC + g * D_STATE:D_SSM + D_BC + (g + 1) * D_STATE]
            c_gb = c_g.astype(MXU_DTYPE)
            cb = _dot_nt(c_gb, b_g)
            ht_g = ht_ref[:, g * gw:(g + 1) * gw]
            y_off = _dot(c_gb, ht_g) * ea_exp[:, g * gw:(g + 1) * gw]
            for hp in range(HEADS_PER_GROUP // 2):
                res = []
                for e in range(2):
                    h = g * HEADS_PER_GROUP + 2 * hp + e
                    seg = a_cs[:, h:h + 1] - a_cs_t[h:h + 1, :]
                    l_mat = jnp.exp(jnp.where(causal, seg, NEG_BIG))
                    gmat = (cb * l_mat * dt_t[h:h + 1, :]).astype(MXU_DTYPE)
                    lo = g * gw + hp * LANES
                    res.append(jnp.dot(gmat, xs_b[:, lo:lo + LANES], preferred_element_type=f32))
                y_pair = jnp.where(lane < SSM_HEAD_DIM, res[0], res[1])
                y_ref[rs, lo:lo + LANES] = y_pair + y_off[:, hp * LANES:(hp + 1) * LANES]
            s_loc = _dot(b_g.T, xw_b[:, g * gw:(g + 1) * gw])
            ht_ref[:, g * gw:(g + 1) * gw] = ht_g * ea_exp[CHUNK - 1:CHUNK, g * gw:(g + 1) * gw] + s_loc

        y = y_ref[rs, :] + dexp_ref[...] * xs
        ys_ref[0, rs, :] = _gate_group_norm(y, z_ref[0, rs, :].astype(f32), nrm_ref[...]).astype(ys_ref.dtype)

    @pl.when(c == pl.num_programs(1) - 1)
    def _():
        for j in range(D_SSM // LANES):
            ssm_ref[0, j * LANES:(j + 1) * LANES, :] = ht_ref[:, j * LANES:(j + 1) * LANES].T


def _ssd(z, xbc, dt_raw, p):
    b, l, _ = z.shape
    n_rows = SSD_CHUNKS_PER_STEP * CHUNK
    tok = lambda i, c: (i, c, 0)
    consts = [p["cwb"], p["cbb"], p["dtb"], p["a"], p["dexp"], p["nrm"], p["e"]]
    f32 = jnp.float32
    return pl.pallas_call(
        _ssd_kernel,
        grid=(b, l // n_rows),
        in_specs=[pl.BlockSpec((1, n_rows, D_SSM), tok), pl.BlockSpec((1, n_rows, D_XBC), tok),
                  pl.BlockSpec((1, n_rows, DT_PAD), tok)] + [_const_spec(a.shape) for a in consts],
        out_specs=[pl.BlockSpec((1, n_rows, D_SSM), tok),
                   pl.BlockSpec((1, D_SSM, D_STATE), lambda i, c: (i, 0, 0))],
        out_shape=[jax.ShapeDtypeStruct((b, l, D_SSM), ACT_DTYPE),
                   jax.ShapeDtypeStruct((b, D_SSM, D_STATE), f32)],
        scratch_shapes=[pltpu.VMEM((B_TAIL + n_rows, D_XBC), f32),
                        pltpu.VMEM((D_STATE, D_SSM), f32),
                        pltpu.VMEM((n_rows, D_XBC), f32),
                        pltpu.VMEM((n_rows, D_SSM), f32)],
        compiler_params=_params("arbitrary", "arbitrary"),
        name="ssd",
    )(z, xbc, dt_raw, *consts)


def _sample_conv_a_kernel(st_ref, u_ref, w_ref, cba_ref, new_ref, ua_ref):
    hist = CONV_A_WIDTH - 1
    for j in range(u_ref.shape[1] // LANES):
        lanes = slice(j * LANES, (j + 1) * LANES)
        u = u_ref[:, lanes]
        acc = u * w_ref[hist:hist + 1, lanes] + cba_ref[:, lanes]
        for k in range(hist):
            row = st_ref[k, :, lanes]
            acc = acc + row * w_ref[k:k + 1, lanes]
            if k >= 1:
                new_ref[k - 1, :, lanes] = row
        new_ref[hist - 1, :, lanes] = u
        ua_ref[:, lanes] = acc


def _sample_conv_a(state_k, u, p):
    hist, s, _ = state_k.shape
    wb = SAMPLE_CONV_LANES
    f32 = jnp.float32
    blk3 = pl.BlockSpec((hist, s, wb), lambda j: (0, 0, j))
    blk2 = pl.BlockSpec((s, wb), lambda j: (0, j))
    return pl.pallas_call(
        _sample_conv_a_kernel,
        grid=(D_CONV // wb,),
        in_specs=[blk3, blk2, pl.BlockSpec((A_TAIL, wb), lambda j: (0, j)), pl.BlockSpec((1, wb), lambda j: (0, j))],
        out_specs=[blk3, blk2],
        out_shape=[jax.ShapeDtypeStruct((hist, s, D_CONV), f32), jax.ShapeDtypeStruct((s, D_CONV), f32)],
        compiler_params=_params("arbitrary"),
        name="sample_conv_a",
    )(state_k, u, p["cwa"], p["cba"])


def _sample_conv_b_kernel(st_ref, xbc_ref, dt_ref, cwb_ref, cbb_ref, dtb_ref, a_ref, e_ref,
                          new_ref, xs_ref, xt_ref, dat_ref, b_ref, ct_ref):
    hist = SSM_CONV_WIDTH - 1
    xbc = xbc_ref[...].astype(jnp.float32)
    acc = xbc * cwb_ref[hist:hist + 1, :] + cbb_ref[...]
    for k in range(hist):
        acc = acc + st_ref[k] * cwb_ref[k:k + 1, :]
    for k in range(hist - 1):
        new_ref[k] = st_ref[k + 1]
    new_ref[hist - 1] = xbc
    xc = _silu(acc)
    xs = xc[:, 0:D_SSM]
    dt = _softplus(dt_ref[...] + dtb_ref[...])
    da = jnp.exp(dt * a_ref[...])
    xs_ref[...] = xs
    x_dt = xs * _dot_split_lhs(dt, e_ref[...])
    for j in range(D_SSM // LANES):
        xt_ref[j * LANES:(j + 1) * LANES, :] = x_dt[:, j * LANES:(j + 1) * LANES].T.astype(xt_ref.dtype)
    dat_ref[...] = da.T[0:SSM_HEADS, :]
    b_ref[...] = xc[:, D_SSM:D_SSM + D_BC]
    for g in range(SSM_GROUPS):
        lo = D_SSM + D_BC + g * D_STATE
        ct_ref[g * D_STATE:(g + 1) * D_STATE, :] = xc[:, lo:lo + D_STATE].T.astype(ct_ref.dtype)


def _sample_conv_b(state_k, xbc, dt_raw, p):
    hist, s, _ = state_k.shape
    f32 = jnp.float32
    args = [state_k, xbc, dt_raw, p["cwb"], p["cbb"], p["dtb"], p["a"], p["e"]]
    shapes = [((hist, s, D_XBC), f32), ((s, D_SSM), f32), ((D_SSM, s), MXU_DTYPE), ((SSM_HEADS, s), f32),
              ((s, D_BC), f32), ((D_BC, s), MXU_DTYPE)]
    return pl.pallas_call(
        _sample_conv_b_kernel,
        grid=(1,),
        in_specs=[_const_spec(a.shape) for a in args],
        out_specs=[_full_spec(sh) for sh, _ in shapes],
        out_shape=[jax.ShapeDtypeStruct(sh, d) for sh, d in shapes],
        compiler_params=_params("arbitrary"),
        name="sample_conv_b",
    )(*args)


def _sample_ssd_kernel(dat_ref, st_ref, xt_ref, b_ref, ct_ref, new_ref, yt_ref):
    i = pl.program_id(0)
    bs = st_ref.shape[0]
    n_seq = yt_ref.shape[1]
    f32 = jnp.float32

    @pl.when(i == 0)
    def _():
        yt_ref[...] = jnp.zeros_like(yt_ref)

    row_id = lax.broadcasted_iota(jnp.int32, (n_seq, D_STATE), 0)
    col_id = lax.broadcasted_iota(jnp.int32, (GROUP_WIDTH, n_seq), 1)

    def body(sl, carry):
        s = i * bs + sl
        for g in range(SSM_GROUPS):
            rows = slice(g * GROUP_WIDTH, (g + 1) * GROUP_WIDTH)
            b_sel = jnp.where(row_id == s, b_ref[:, g * D_STATE:(g + 1) * D_STATE], 0.0).astype(MXU_DTYPE)
            outer = jnp.dot(xt_ref[rows, :], b_sel, preferred_element_type=f32)
            h_new = []
            for hh in range(HEADS_PER_GROUP):
                h = g * HEADS_PER_GROUP + hh
                hn = st_ref[sl, h] * dat_ref[h, s] + outer[hh * SSM_HEAD_DIM:(hh + 1) * SSM_HEAD_DIM, :]
                new_ref[sl, h] = hn
                h_new.append(hn.astype(MXU_DTYPE))
            y_all = jnp.dot(jnp.concatenate(h_new, axis=0), ct_ref[g * D_STATE:(g + 1) * D_STATE, :],
                            preferred_element_type=f32)
            yt_ref[rows, :] = jnp.where(col_id == s, y_all, yt_ref[rows, :])
        return carry

    lax.fori_loop(0, bs, body, 0, unroll=2)


def _sample_ssd(state, xt, dat, b, ct):
    s = state.shape[0]
    bs = SAMPLE_SSD_BLOCK
    blk = pl.BlockSpec((bs, SSM_HEADS, SSM_HEAD_DIM, D_STATE), lambda i: (i, 0, 0, 0))
    return pl.pallas_call(
        _sample_ssd_kernel,
        grid=(s // bs,),
        in_specs=[pl.BlockSpec(memory_space=pltpu.SMEM), blk,
                  _const_spec(xt.shape), _const_spec(b.shape), _const_spec(ct.shape)],
        out_specs=[blk, _full_spec((D_SSM, s))],
        out_shape=[jax.ShapeDtypeStruct(state.shape, jnp.float32), jax.ShapeDtypeStruct((D_SSM, s), jnp.float32)],
        compiler_params=_params("arbitrary"),
        name="sample_ssd",
    )(dat, state, xt, b, ct)


def _sample_gate_kernel(ua_ref, lng_ref, lnb_ref, yt_ref, xs_ref, z_ref, dexp_ref, nrm_ref, ya_ref, ys_ref):
    ya_ref[...] = _layer_norm_silu(ua_ref[...], lng_ref[...], lnb_ref[...]).astype(ya_ref.dtype)
    y = jnp.concatenate([yt_ref[j * LANES:(j + 1) * LANES, :].T for j in range(D_SSM // LANES)], axis=-1)
    y = y + dexp_ref[...] * xs_ref[...]
    ys_ref[...] = _gate_group_norm(y, z_ref[...].astype(jnp.float32), nrm_ref[...]).astype(ys_ref.dtype)


def _sample_gate(ua, yt, xs, z, p):
    s = xs.shape[0]
    args = [ua, p["lng"], p["lnb"], yt, xs, z, p["dexp"], p["nrm"]]
    return pl.pallas_call(
        _sample_gate_kernel,
        grid=(1,),
        in_specs=[_const_spec(a.shape) for a in args],
        out_specs=[_full_spec((s, D_CONV)), _full_spec((s, D_SSM))],
        out_shape=[jax.ShapeDtypeStruct((s, D_CONV), ACT_DTYPE), jax.ShapeDtypeStruct((s, D_SSM), ACT_DTYPE)],
        compiler_params=_params("arbitrary"),
        name="sample_gate",
    )(*args)


def _prep_params(i, conv_dw_w, conv_dw_b, conv_ln_g, conv_ln_b, ssm_conv_w, ssm_conv_b, ssm_dt_bias, ssm_a_log,
                 ssm_d, ssm_norm):
    f32 = jnp.float32
    row = lambda v: v.reshape(1, -1).astype(f32)
    pad_heads = lambda v: jnp.pad(v.astype(f32), (0, DT_PAD - SSM_HEADS)).reshape(1, DT_PAD)
    head_of_channel = jnp.arange(D_SSM, dtype=jnp.int32) // SSM_HEAD_DIM
    expand = (jnp.arange(DT_PAD, dtype=jnp.int32)[:, None] == head_of_channel[None, :]).astype(MXU_DTYPE)
    return {
        "cwa": jnp.pad(conv_dw_w[i].astype(f32), ((0, A_TAIL - CONV_A_WIDTH), (0, 0))),
        "cba": row(conv_dw_b[i]), "lng": row(conv_ln_g[i]), "lnb": row(conv_ln_b[i]),
        "cwb": jnp.pad(ssm_conv_w[i].astype(f32), ((0, B_TAIL - SSM_CONV_WIDTH), (0, 0))),
        "cbb": row(ssm_conv_b[i]),
        "dtb": pad_heads(ssm_dt_bias[i]),
        "a": pad_heads(-jnp.exp(ssm_a_log[i].astype(f32))),
        "dexp": jnp.repeat(ssm_d[i].astype(f32), SSM_HEAD_DIM).reshape(1, D_SSM),
        "nrm": row(ssm_norm[i]),
        "e": expand,
    }


def kernel(x_prompt, x_sample, state_conv_a, state_conv_b, state_ssm, ffn1_norm, ffn1_w_gate, ffn1_w_up, ffn1_w_down, mix_norm, w_in, conv_dw_w, conv_dw_b, conv_ln_g, conv_ln_b, ssm_conv_w, ssm_conv_b, ssm_dt_bias, ssm_a_log, ssm_d, ssm_norm, w_out, ffn2_norm, ffn2_w_gate, ffn2_w_up, ffn2_w_down, final_norm):
    depth = ffn1_norm.shape[0]
    bp, lp, _ = x_prompt.shape
    bs = x_sample.shape[0]
    f32 = jnp.float32
    row = lambda v: v.reshape(1, -1).astype(f32)
    w16 = lambda w: w.astype(MXU_DTYPE)

    yp = x_prompt.reshape(bp * lp, D_MODEL)
    ysm = x_sample.reshape(bs, D_MODEL)
    outs = [[] for _ in range(6)]
    for i in range(depth):
        p = _prep_params(i, conv_dw_w, conv_dw_b, conv_ln_g, conv_ln_b, ssm_conv_w, ssm_conv_b, ssm_dt_bias,
                         ssm_a_log, ssm_d, ssm_norm)
        w_in_t = jnp.transpose(w_in[i])
        n_main = 2 * D_CONV + D_SSM + D_XBC
        w_dt = jnp.pad(w_in_t[n_main:], ((0, DT_PAD - SSM_HEADS), (0, 0)))
        f1 = (row(ffn1_norm[i]), w16(ffn1_w_gate[i]), w16(ffn1_w_up[i]), w16(ffn1_w_down[i]))
        fin = row(final_norm) if i == depth - 1 else None

        x1, w2g, w2u, w2d, wo, w_main = _ffn(
            yp, *f1, cast=(ffn2_w_gate[i], ffn2_w_up[i], ffn2_w_down[i], w_out[i]), cast_t=((w_in_t, n_main),))
        f2 = (row(ffn2_norm[i]), w2g, w2u, w2d)
        ya, a_tail, z, xbc, dt_raw = _inproj_conv(x1, row(mix_norm[i]), w_main, w_dt,
                                                  (p["cwa"], p["cba"], p["lng"], p["lnb"]), lp)
        xbc3 = xbc.reshape(bp, lp, D_XBC)
        ys, ssm_t = _ssd(z.reshape(bp, lp, D_SSM), xbc3, dt_raw.reshape(bp, lp, DT_PAD), p)
        yp = _ffn(x1, *f2, ya=ya, ys=ys.reshape(bp * lp, D_SSM), wo=wo, final_g=fin)
        outs[0].append(a_tail[:, A_TAIL - (CONV_A_WIDTH - 1):, :].astype(x_prompt.dtype))
        outs[1].append(xbc3[:, lp - (SSM_CONV_WIDTH - 1):, :].astype(x_prompt.dtype))
        outs[2].append(ssm_t.reshape(bp, SSM_HEADS, SSM_HEAD_DIM, D_STATE).astype(state_ssm.dtype))

        s1 = _ffn(ysm, *f1)
        su, sz, sxbc, sdt = _inproj(s1, row(mix_norm[i]), w_main, w_dt)
        tap_major = lambda st: jnp.transpose(st.astype(f32), (1, 0, 2))
        new_a, sua = _sample_conv_a(tap_major(state_conv_a[i]), su, p)
        new_b, xs, xt, dat, b_mat, ct = _sample_conv_b(tap_major(state_conv_b[i]), sxbc, sdt, p)
        new_s, yt = _sample_ssd(state_ssm[i].astype(f32), xt, dat, b_mat, ct)
        sya, sys_ = _sample_gate(sua, yt, xs, sz, p)
        ysm = _ffn(s1, *f2, ya=sya, ys=sys_, wo=wo, final_g=fin)
        outs[3].append(jnp.transpose(new_a, (1, 0, 2)).astype(x_sample.dtype))
        outs[4].append(jnp.transpose(new_b, (1, 0, 2)).astype(x_sample.dtype))
        outs[5].append(new_s.astype(state_ssm.dtype))

    return (yp.reshape(bp, lp, D_MODEL), ysm.reshape(bs, 1, D_MODEL)) + tuple(jnp.stack(o) for o in outs)
```

```python
import functools

import jax
import jax.numpy as jnp
from jax import lax
from jax.experimental import pallas as pl
from jax.experimental.pallas import tpu as pltpu

D_MODEL = 1024
D_FF = 2816
D_CONV = 1024
CONV_A_WIDTH = 31
D_SSM = 1024
SSM_HEAD_DIM = 64
SSM_HEADS = 16
SSM_GROUPS = 2
HEADS_PER_GROUP = SSM_HEADS // SSM_GROUPS
GROUP_WIDTH = HEADS_PER_GROUP * SSM_HEAD_DIM
D_STATE = 128
SSM_CONV_WIDTH = 4
CHUNK = 128
D_XBC = D_SSM + 2 * SSM_GROUPS * D_STATE
D_BC = SSM_GROUPS * D_STATE
FFN_RES_WEIGHT = 0.5
NORM_EPS = 1e-5

LANES = 128
SUBLANES = 8
DT_PAD = LANES
A_TAIL = 32
B_TAIL = 8
ROW_TILE = 512
CONV_ROWS = 128
SSD_CHUNKS_PER_STEP = 8
SAMPLE_SSD_BLOCK = 16
SAMPLE_CONV_LANES = 256
CAST_T_BLOCK_ROWS = 256
VMEM_LIMIT = 56 * 1024 * 1024
NEG_BIG = -1e30

ACT_DTYPE = jnp.bfloat16
MXU_DTYPE = jnp.bfloat16


def _dot(a, b):
    return jnp.dot(a.astype(MXU_DTYPE), b.astype(MXU_DTYPE), preferred_element_type=jnp.float32)


def _dot_nt(a, b):
    return lax.dot_general(a.astype(MXU_DTYPE), b.astype(MXU_DTYPE), (((1,), (1,)), ((), ())),
                           preferred_element_type=jnp.float32)


def _split2(v):
    hi = v.astype(MXU_DTYPE)
    lo = (v - hi.astype(jnp.float32)).astype(MXU_DTYPE)
    return hi, lo


def _dot_split_lhs(v, m):
    hi, lo = _split2(v)
    return (jnp.dot(hi, m, preferred_element_type=jnp.float32)
            + jnp.dot(lo, m, preferred_element_type=jnp.float32))


def _dot_split_rhs(m, v):
    hi = v.astype(MXU_DTYPE)
    r1 = v - hi.astype(jnp.float32)
    mid = r1.astype(MXU_DTYPE)
    lo = (r1 - mid.astype(jnp.float32)).astype(MXU_DTYPE)
    return (jnp.dot(m, hi, preferred_element_type=jnp.float32)
            + jnp.dot(m, mid, preferred_element_type=jnp.float32)
            + jnp.dot(m, lo, preferred_element_type=jnp.float32))


def _rms_norm(x, g):
    return x * lax.rsqrt(jnp.mean(x * x, axis=-1, keepdims=True) + NORM_EPS) * g


def _silu(x):
    return x * jax.nn.sigmoid(x)


def _softplus(x):
    return jnp.maximum(x, 0.0) + jnp.log1p(jnp.exp(-jnp.abs(x)))


def _layer_norm_silu(x, g, b):
    mu = jnp.mean(x, axis=-1, keepdims=True)
    xc = x - mu
    var = jnp.mean(xc * xc, axis=-1, keepdims=True)
    return _silu(xc * lax.rsqrt(var + NORM_EPS) * g + b)


def _gate_group_norm(y, z, g):
    y = y * _silu(z)
    parts = []
    for i in range(SSM_GROUPS):
        yg = y[:, i * GROUP_WIDTH:(i + 1) * GROUP_WIDTH]
        parts.append(yg * lax.rsqrt(jnp.mean(yg * yg, axis=-1, keepdims=True) + NORM_EPS))
    return jnp.concatenate(parts, axis=-1) * g


def _causal_conv_tile(ext_ref, w_ref, lanes, row0, n_out, tail, width):
    n_rows = tail + n_out
    x = ext_ref[row0:row0 + n_rows, lanes]
    first = tail - (width - 1)
    acc = jnp.zeros((n_out, LANES), jnp.float32)
    for r in range(SUBLANES):
        taps = [k for k in range(width) if (first + k) % SUBLANES == r]
        if not taps:
            continue
        xr = x if r == 0 else pltpu.roll(x, n_rows - r, 0)
        for k in taps:
            base = first + k - r
            acc = acc + xr[base:base + n_out, :] * w_ref[k:k + 1, lanes]
    return acc


def _const_spec(shape):
    nd = len(shape)
    return pl.BlockSpec(shape, lambda *_: (0,) * nd, pipeline_mode=pl.Buffered(1))


def _full_spec(shape):
    nd = len(shape)
    return pl.BlockSpec(shape, lambda *_: (0,) * nd)


def _params(*semantics):
    return pltpu.CompilerParams(dimension_semantics=semantics, vmem_limit_bytes=VMEM_LIMIT)


def _ffn_kernel(*refs, with_outproj, with_final_norm, n_cast, n_cast_t):
    refs = list(refs)
    x_ref = refs.pop(0)
    if with_outproj:
        ya_ref, ys_ref, wo_ref = refs[:3]
        refs = refs[3:]
    g_ref, wg_ref, wu_ref, wd_ref = refs[:4]
    refs = refs[4:]
    if with_final_norm:
        fg_ref = refs.pop(0)
    n_jobs = n_cast + n_cast_t
    cast_in = refs[:n_jobs]
    o_ref = refs[n_jobs]
    cast_out = refs[n_jobs + 1:]
    for src, dst in zip(cast_in[:n_cast], cast_out[:n_cast]):
        dst[...] = src[...].astype(dst.dtype)
    for src, dst in zip(cast_in[n_cast:], cast_out[n_cast:]):
        for a in range(src.shape[0] // LANES):
            for b in range(src.shape[1] // LANES):
                dst[b * LANES:(b + 1) * LANES, a * LANES:(a + 1) * LANES] = (
                    src[a * LANES:(a + 1) * LANES, b * LANES:(b + 1) * LANES].T.astype(dst.dtype))

    x = x_ref[...]
    if with_outproj:
        x = x + _dot(ya_ref[...], wo_ref[0:D_CONV, :]) + _dot(ys_ref[...], wo_ref[D_CONV:D_CONV + D_SSM, :])
    xn = _rms_norm(x, g_ref[...]).astype(MXU_DTYPE)
    gate = jnp.dot(xn, wg_ref[...], preferred_element_type=jnp.float32)
    up = jnp.dot(xn, wu_ref[...], preferred_element_type=jnp.float32)
    h = (_silu(gate) * up).astype(MXU_DTYPE)
    y = x + FFN_RES_WEIGHT * jnp.dot(h, wd_ref[...], preferred_element_type=jnp.float32)
    if with_final_norm:
        y = _rms_norm(y, fg_ref[...])
    o_ref[...] = y


def _cast_block_rows(n_rows, n_steps):
    tile = 2 * SUBLANES
    n_blocks = max(d for d in range(1, n_steps + 1) if n_rows % d == 0 and (n_rows // d) % tile == 0)
    return n_rows // n_blocks


def _ffn(x, norm_g, wg, wu, wd, ya=None, ys=None, wo=None, final_g=None, cast=(), cast_t=()):
    m = x.shape[0]
    tm = min(m, ROW_TILE)
    n_steps = m // tm
    with_outproj = ya is not None
    with_final_norm = final_g is not None
    row = lambda i: (i, 0)
    args, specs = [x], [pl.BlockSpec((tm, D_MODEL), row)]
    if with_outproj:
        args += [ya, ys, wo]
        specs += [pl.BlockSpec((tm, D_CONV), row), pl.BlockSpec((tm, D_SSM), row), _const_spec(wo.shape)]
    args += [norm_g, wg, wu, wd]
    specs += [_const_spec(norm_g.shape), _const_spec(wg.shape), _const_spec(wu.shape), _const_spec(wd.shape)]
    if with_final_norm:
        args.append(final_g)
        specs.append(_const_spec(final_g.shape))
    cast_specs = []
    for w in cast:
        br = _cast_block_rows(w.shape[0], n_steps)
        last = w.shape[0] // br - 1
        cast_specs.append(pl.BlockSpec((br, w.shape[1]), lambda i, last=last: (jnp.minimum(i, last), 0)))
    out_specs = list(cast_specs)
    out_shapes = [jax.ShapeDtypeStruct(w.shape, MXU_DTYPE) for w in cast]
    for w, n_rows in cast_t:
        br = CAST_T_BLOCK_ROWS
        last = n_rows // br - 1
        assert n_rows % br == 0 and last < n_steps and w.shape[1] % LANES == 0
        cast_specs.append(pl.BlockSpec((br, w.shape[1]), lambda i, last=last: (jnp.minimum(i, last), 0)))
        out_specs.append(pl.BlockSpec((w.shape[1], br), lambda i, last=last: (0, jnp.minimum(i, last))))
        out_shapes.append(jax.ShapeDtypeStruct((w.shape[1], n_rows), MXU_DTYPE))
    out = pl.pallas_call(
        functools.partial(_ffn_kernel, with_outproj=with_outproj, with_final_norm=with_final_norm,
                          n_cast=len(cast), n_cast_t=len(cast_t)),
        grid=(n_steps,),
        in_specs=specs + cast_specs,
        out_specs=[pl.BlockSpec((tm, D_MODEL), row)] + out_specs,
        out_shape=[jax.ShapeDtypeStruct((m, D_MODEL), jnp.float32)] + out_shapes,
        compiler_params=_params("arbitrary"),
        name="ffn_out" if with_outproj else "ffn_in",
    )(*args, *cast, *[w for w, _ in cast_t])
    return out if (cast or cast_t) else out[0]


def _project(x_ref, g_ref, w_ref, wdt_ref, z_ref, xbc_ref, dt_ref):
    xn = _rms_norm(x_ref[...], g_ref[...]).astype(MXU_DTYPE)

    def proj(lo, width):
        return jnp.dot(xn, w_ref[:, lo:lo + width], preferred_element_type=jnp.float32)

    u = proj(0, D_CONV) * jax.nn.sigmoid(proj(D_CONV, D_CONV))
    z_ref[...] = proj(2 * D_CONV, D_SSM).astype(z_ref.dtype)
    xbc_ref[...] = proj(2 * D_CONV + D_SSM, D_XBC).astype(xbc_ref.dtype)
    dt_ref[...] = _dot_nt(xn, wdt_ref[...])
    return u


def _inproj_kernel(x_ref, g_ref, w_ref, wdt_ref, u_ref, z_ref, xbc_ref, dt_ref):
    u_ref[...] = _project(x_ref, g_ref, w_ref, wdt_ref, z_ref, xbc_ref, dt_ref)


def _inproj_conv_kernel(x_ref, g_ref, w_ref, wdt_ref, cwa_ref, cba_ref, lng_ref, lnb_ref,
                        ya_ref, tail_ref, z_ref, xbc_ref, dt_ref, ext_ref, ua_ref, *, tiles_per_seq):
    tm = x_ref.shape[0]

    @pl.when(pl.program_id(0) % tiles_per_seq == 0)
    def _():
        ext_ref[0:A_TAIL, :] = jnp.zeros((A_TAIL, D_CONV), jnp.float32)

    ext_ref[A_TAIL:A_TAIL + tm, :] = _project(x_ref, g_ref, w_ref, wdt_ref, z_ref, xbc_ref, dt_ref)
    for c in range(tm // CONV_ROWS):
        for j in range(D_CONV // LANES):
            lanes = slice(j * LANES, (j + 1) * LANES)
            ua_ref[c * CONV_ROWS:(c + 1) * CONV_ROWS, lanes] = (
                _causal_conv_tile(ext_ref, cwa_ref, lanes, c * CONV_ROWS, CONV_ROWS, A_TAIL, CONV_A_WIDTH)
                + cba_ref[:, lanes])
    tail = ext_ref[tm:tm + A_TAIL, :]
    tail_ref[0] = tail
    ext_ref[0:A_TAIL, :] = tail
    ya_ref[...] = _layer_norm_silu(ua_ref[...], lng_ref[...], lnb_ref[...]).astype(ya_ref.dtype)


def _inproj(x, norm_g, w_main, w_dt):
    m = x.shape[0]
    tm = min(m, ROW_TILE)
    row = lambda i: (i, 0)
    widths = (D_CONV, D_SSM, D_XBC, DT_PAD)
    dtypes = (jnp.float32, ACT_DTYPE, ACT_DTYPE, jnp.float32)
    return pl.pallas_call(
        _inproj_kernel,
        grid=(m // tm,),
        in_specs=[pl.BlockSpec((tm, D_MODEL), row), _const_spec(norm_g.shape), _const_spec(w_main.shape),
                  _const_spec(w_dt.shape)],
        out_specs=[pl.BlockSpec((tm, w), row) for w in widths],
        out_shape=[jax.ShapeDtypeStruct((m, w), d) for w, d in zip(widths, dtypes)],
        compiler_params=_params("arbitrary"),
        name="inproj",
    )(x, norm_g, w_main, w_dt)


def _inproj_conv(x, norm_g, w_main, w_dt, conv, seq_len):
    m = x.shape[0]
    tm = min(seq_len, ROW_TILE)
    tiles_per_seq = seq_len // tm
    f32 = jnp.float32
    row = lambda i: (i, 0)
    return pl.pallas_call(
        functools.partial(_inproj_conv_kernel, tiles_per_seq=tiles_per_seq),
        grid=(m // tm,),
        in_specs=[pl.BlockSpec((tm, D_MODEL), row), _const_spec(norm_g.shape), _const_spec(w_main.shape),
                  _const_spec(w_dt.shape)] + [_const_spec(a.shape) for a in conv],
        out_specs=[pl.BlockSpec((tm, D_CONV), row),
                   pl.BlockSpec((1, A_TAIL, D_CONV), lambda i: (i // tiles_per_seq, 0, 0))]
                  + [pl.BlockSpec((tm, w), row) for w in (D_SSM, D_XBC, DT_PAD)],
        out_shape=[jax.ShapeDtypeStruct((m, D_CONV), ACT_DTYPE),
                   jax.ShapeDtypeStruct((m // seq_len, A_TAIL, D_CONV), f32),
                   jax.ShapeDtypeStruct((m, D_SSM), ACT_DTYPE), jax.ShapeDtypeStruct((m, D_XBC), ACT_DTYPE),
                   jax.ShapeDtypeStruct((m, DT_PAD), f32)],
        scratch_shapes=[pltpu.VMEM((A_TAIL + tm, D_CONV), f32), pltpu.VMEM((tm, D_CONV), f32)],
        compiler_params=_params("arbitrary"),
        name="inproj_conv",
    )(x, norm_g, w_main, w_dt, *conv)


def _ssd_kernel(z_ref, xbc_ref, dt_ref,
                cwb_ref, cbb_ref, dtb_ref, a_ref, dexp_ref, nrm_ref, e_ref,
                ys_ref, ssm_ref,
                extb_ref, ht_ref, xc_ref, y_ref):
    c = pl.program_id(1)
    f32 = jnp.float32
    n_rows = SSD_CHUNKS_PER_STEP * CHUNK

    @pl.when(c == 0)
    def _():
        extb_ref[0:B_TAIL, :] = jnp.zeros((B_TAIL, D_XBC), f32)
        ht_ref[...] = jnp.zeros_like(ht_ref)

    extb_ref[B_TAIL:B_TAIL + n_rows, :] = xbc_ref[0].astype(f32)
    for q in range(SSD_CHUNKS_PER_STEP):
        for j in range(D_XBC // LANES):
            lanes = slice(j * LANES, (j + 1) * LANES)
            xc_ref[q * CHUNK:(q + 1) * CHUNK, lanes] = _silu(
                _causal_conv_tile(extb_ref, cwb_ref, lanes, q * CHUNK, CHUNK, B_TAIL, SSM_CONV_WIDTH)
                + cbb_ref[:, lanes])
    extb_ref[0:B_TAIL, :] = extb_ref[n_rows:n_rows + B_TAIL, :]

    rows = lax.broadcasted_iota(jnp.int32, (CHUNK, CHUNK), 0)
    cols = lax.broadcasted_iota(jnp.int32, (CHUNK, CHUNK), 1)
    causal = rows >= cols
    tri = causal.astype(MXU_DTYPE)
    lane = lax.broadcasted_iota(jnp.int32, (CHUNK, LANES), 1)
    e_mat = e_ref[...]
    gw = GROUP_WIDTH
    for q in range(SSD_CHUNKS_PER_STEP):
        rs = slice(q * CHUNK, (q + 1) * CHUNK)
        dt = _softplus(dt_ref[0, rs, :] + dtb_ref[...])
        a = dt * a_ref[...]
        a_cs = _dot_split_rhs(tri, a)
        a_cs_t = a_cs.T
        dt_t = dt.T
        ea = jnp.exp(a_cs)
        w_end = jnp.exp(a_cs[CHUNK - 1:CHUNK, :] - a_cs) * dt
        w_exp = _dot_split_lhs(w_end, e_mat)
        ea_exp = _dot_split_lhs(ea, e_mat)
        xs = xc_ref[rs, 0:D_SSM]
        xs_b = xs.astype(MXU_DTYPE)
        xw_b = (xs * w_exp).astype(MXU_DTYPE)
        for g in range(SSM_GROUPS):
            b_g = xc_ref[rs, D_SSM + g * D_STATE:D_SSM + (g + 1) * D_STATE]
            c_g = xc_ref[rs, D_SSM + D_BC + g * D_STATE:D_SSM + D_BC + (g + 1) * D_STATE]
            c_gb = c_g.astype(MXU_DTYPE)
            cb = _dot_nt(c_gb, b_g)
            ht_g = ht_ref[:, g * gw:(g + 1) * gw]
            y_off = _dot(c_gb, ht_g) * ea_exp[:, g * gw:(g + 1) * gw]
            for hp in range(HEADS_PER_GROUP // 2):
                res = []
                for e in range(2):
                    h = g * HEADS_PER_GROUP + 2 * hp + e
                    seg = a_cs[:, h:h + 1] - a_cs_t[h:h + 1, :]
                    l_mat = jnp.exp(jnp.where(causal, seg, NEG_BIG))
                    gmat = (cb * l_mat * dt_t[h:h + 1, :]).astype(MXU_DTYPE)
                    lo = g * gw + hp * LANES
                    res.append(jnp.dot(gmat, xs_b[:, lo:lo + LANES], preferred_element_type=f32))
                y_pair = jnp.where(lane < SSM_HEAD_DIM, res[0], res[1])
                y_ref[rs, lo:lo + LANES] = y_pair + y_off[:, hp * LANES:(hp + 1) * LANES]
            s_loc = _dot(b_g.T, xw_b[:, g * gw:(g + 1) * gw])
            ht_ref[:, g * gw:(g + 1) * gw] = ht_g * ea_exp[CHUNK - 1:CHUNK, g * gw:(g + 1) * gw] + s_loc

        y = y_ref[rs, :] + dexp_ref[...] * xs
        ys_ref[0, rs, :] = _gate_group_norm(y, z_ref[0, rs, :].astype(f32), nrm_ref[...]).astype(ys_ref.dtype)

    @pl.when(c == pl.num_programs(1) - 1)
    def _():
        for j in range(D_SSM // LANES):
            ssm_ref[0, j * LANES:(j + 1) * LANES, :] = ht_ref[:, j * LANES:(j + 1) * LANES].T


def _ssd(z, xbc, dt_raw, p):
    b, l, _ = z.shape
    n_rows = SSD_CHUNKS_PER_STEP * CHUNK
    tok = lambda i, c: (i, c, 0)
    consts = [p["cwb"], p["cbb"], p["dtb"], p["a"], p["dexp"], p["nrm"], p["e"]]
    f32 = jnp.float32
    return pl.pallas_call(
        _ssd_kernel,
        grid=(b, l // n_rows),
        in_specs=[pl.BlockSpec((1, n_rows, D_SSM), tok), pl.BlockSpec((1, n_rows, D_XBC), tok),
                  pl.BlockSpec((1, n_rows, DT_PAD), tok)] + [_const_spec(a.shape) for a in consts],
        out_specs=[pl.BlockSpec((1, n_rows, D_SSM), tok),
                   pl.BlockSpec((1, D_SSM, D_STATE), lambda i, c: (i, 0, 0))],
        out_shape=[jax.ShapeDtypeStruct((b, l, D_SSM), ACT_DTYPE),
                   jax.ShapeDtypeStruct((b, D_SSM, D_STATE), f32)],
        scratch_shapes=[pltpu.VMEM((B_TAIL + n_rows, D_XBC), f32),
                        pltpu.VMEM((D_STATE, D_SSM), f32),
                        pltpu.VMEM((n_rows, D_XBC), f32),
                        pltpu.VMEM((n_rows, D_SSM), f32)],
        compiler_params=_params("arbitrary", "arbitrary"),
        name="ssd",
    )(z, xbc, dt_raw, *consts)


def _sample_conv_a_kernel(st_ref, u_ref, w_ref, cba_ref, new_ref, ua_ref):
    hist = CONV_A_WIDTH - 1
    for j in range(u_ref.shape[1] // LANES):
        lanes = slice(j * LANES, (j + 1) * LANES)
        u = u_ref[:, lanes]
        acc = u * w_ref[hist:hist + 1, lanes] + cba_ref[:, lanes]
        for k in range(hist):
            row = st_ref[k, :, lanes]
            acc = acc + row * w_ref[k:k + 1, lanes]
            if k >= 1:
                new_ref[k - 1, :, lanes] = row
        new_ref[hist - 1, :, lanes] = u
        ua_ref[:, lanes] = acc


def _sample_conv_a(state_k, u, p):
    hist, s, _ = state_k.shape
    wb = SAMPLE_CONV_LANES
    f32 = jnp.float32
    blk3 = pl.BlockSpec((hist, s, wb), lambda j: (0, 0, j))
    blk2 = pl.BlockSpec((s, wb), lambda j: (0, j))
    return pl.pallas_call(
        _sample_conv_a_kernel,
        grid=(D_CONV // wb,),
        in_specs=[blk3, blk2, pl.BlockSpec((A_TAIL, wb), lambda j: (0, j)), pl.BlockSpec((1, wb), lambda j: (0, j))],
        out_specs=[blk3, blk2],
        out_shape=[jax.ShapeDtypeStruct((hist, s, D_CONV), f32), jax.ShapeDtypeStruct((s, D_CONV), f32)],
        compiler_params=_params("arbitrary"),
        name="sample_conv_a",
    )(state_k, u, p["cwa"], p["cba"])


def _sample_conv_b_kernel(st_ref, xbc_ref, dt_ref, cwb_ref, cbb_ref, dtb_ref, a_ref, e_ref,
                          new_ref, xs_ref, xt_ref, dat_ref, b_ref, ct_ref):
    hist = SSM_CONV_WIDTH - 1
    xbc = xbc_ref[...].astype(jnp.float32)
    acc = xbc * cwb_ref[hist:hist + 1, :] + cbb_ref[...]
    for k in range(hist):
        acc = acc + st_ref[k] * cwb_ref[k:k + 1, :]
    for k in range(hist - 1):
        new_ref[k] = st_ref[k + 1]
    new_ref[hist - 1] = xbc
    xc = _silu(acc)
    xs = xc[:, 0:D_SSM]
    dt = _softplus(dt_ref[...] + dtb_ref[...])
    da = jnp.exp(dt * a_ref[...])
    xs_ref[...] = xs
    x_dt = xs * _dot_split_lhs(dt, e_ref[...])
    for j in range(D_SSM // LANES):
        xt_ref[j * LANES:(j + 1) * LANES, :] = x_dt[:, j * LANES:(j + 1) * LANES].T.astype(xt_ref.dtype)
    dat_ref[...] = da.T[0:SSM_HEADS, :]
    b_ref[...] = xc[:, D_SSM:D_SSM + D_BC]
    for g in range(SSM_GROUPS):
        lo = D_SSM + D_BC + g * D_STATE
        ct_ref[g * D_STATE:(g + 1) * D_STATE, :] = xc[:, lo:lo + D_STATE].T.astype(ct_ref.dtype)


def _sample_conv_b(state_k, xbc, dt_raw, p):
    hist, s, _ = state_k.shape
    f32 = jnp.float32
    args = [state_k, xbc, dt_raw, p["cwb"], p["cbb"], p["dtb"], p["a"], p["e"]]
    shapes = [((hist, s, D_XBC), f32), ((s, D_SSM), f32), ((D_SSM, s), MXU_DTYPE), ((SSM_HEADS, s), f32),
              ((s, D_BC), f32), ((D_BC, s), MXU_DTYPE)]
    return pl.pallas_call(
        _sample_conv_b_kernel,
        grid=(1,),
        in_specs=[_const_spec(a.shape) for a in args],
        out_specs=[_full_spec(sh) for sh, _ in shapes],
        out_shape=[jax.ShapeDtypeStruct(sh, d) for sh, d in shapes],
        compiler_params=_params("arbitrary"),
        name="sample_conv_b",
    )(*args)


def _sample_ssd_kernel(dat_ref, st_ref, xt_ref, b_ref, ct_ref, new_ref, yt_ref):
    i = pl.program_id(0)
    bs = st_ref.shape[0]
    n_seq = yt_ref.shape[1]
    f32 = jnp.float32

    @pl.when(i == 0)
    def _():
        yt_ref[...] = jnp.zeros_like(yt_ref)

    row_id = lax.broadcasted_iota(jnp.int32, (n_seq, D_STATE), 0)
    col_id = lax.broadcasted_iota(jnp.int32, (GROUP_WIDTH, n_seq), 1)

    def body(sl, carry):
        s = i * bs + sl
        for g in range(SSM_GROUPS):
            rows = slice(g * GROUP_WIDTH, (g + 1) * GROUP_WIDTH)
            b_sel = jnp.where(row_id == s, b_ref[:, g * D_STATE:(g + 1) * D_STATE], 0.0).astype(MXU_DTYPE)
            outer = jnp.dot(xt_ref[rows, :], b_sel, preferred_element_type=f32)
            h_new = []
            for hh in range(HEADS_PER_GROUP):
                h = g * HEADS_PER_GROUP + hh
                hn = st_ref[sl, h] * dat_ref[h, s] + outer[hh * SSM_HEAD_DIM:(hh + 1) * SSM_HEAD_DIM, :]
                new_ref[sl, h] = hn
                h_new.append(hn.astype(MXU_DTYPE))
            y_all = jnp.dot(jnp.concatenate(h_new, axis=0), ct_ref[g * D_STATE:(g + 1) * D_STATE, :],
                            preferred_element_type=f32)
            yt_ref[rows, :] = jnp.where(col_id == s, y_all, yt_ref[rows, :])
        return carry

    lax.fori_loop(0, bs, body, 0, unroll=2)


def _sample_ssd(state, xt, dat, b, ct):
    s = state.shape[0]
    bs = SAMPLE_SSD_BLOCK
    blk = pl.BlockSpec((bs, SSM_HEADS, SSM_HEAD_DIM, D_STATE), lambda i: (i, 0, 0, 0))
    return pl.pallas_call(
        _sample_ssd_kernel,
        grid=(s // bs,),
        in_specs=[pl.BlockSpec(memory_space=pltpu.SMEM), blk,
                  _const_spec(xt.shape), _const_spec(b.shape), _const_spec(ct.shape)],
        out_specs=[blk, _full_spec((D_SSM, s))],
        out_shape=[jax.ShapeDtypeStruct(state.shape, jnp.float32), jax.ShapeDtypeStruct((D_SSM, s), jnp.float32)],
        compiler_params=_params("arbitrary"),
        name="sample_ssd",
    )(dat, state, xt, b, ct)


def _sample_gate_kernel(ua_ref, lng_ref, lnb_ref, yt_ref, xs_ref, z_ref, dexp_ref, nrm_ref, ya_ref, ys_ref):
    ya_ref[...] = _layer_norm_silu(ua_ref[...], lng_ref[...], lnb_ref[...]).astype(ya_ref.dtype)
    y = jnp.concatenate([yt_ref[j * LANES:(j + 1) * LANES, :].T for j in range(D_SSM // LANES)], axis=-1)
    y = y + dexp_ref[...] * xs_ref[...]
    ys_ref[...] = _gate_group_norm(y, z_ref[...].astype(jnp.float32), nrm_ref[...]).astype(ys_ref.dtype)


def _sample_gate(ua, yt, xs, z, p):
    s = xs.shape[0]
    args = [ua, p["lng"], p["lnb"], yt, xs, z, p["dexp"], p["nrm"]]
    return pl.pallas_call(
        _sample_gate_kernel,
        grid=(1,),
        in_specs=[_const_spec(a.shape) for a in args],
        out_specs=[_full_spec((s, D_CONV)), _full_spec((s, D_SSM))],
        out_shape=[jax.ShapeDtypeStruct((s, D_CONV), ACT_DTYPE), jax.ShapeDtypeStruct((s, D_SSM), ACT_DTYPE)],
        compiler_params=_params("arbitrary"),
        name="sample_gate",
    )(*args)


def _prep_params(i, conv_dw_w, conv_dw_b, conv_ln_g, conv_ln_b, ssm_conv_w, ssm_conv_b, ssm_dt_bias, ssm_a_log,
                 ssm_d, ssm_norm):
    f32 = jnp.float32
    row = lambda v: v.reshape(1, -1).astype(f32)
    pad_heads = lambda v: jnp.pad(v.astype(f32), (0, DT_PAD - SSM_HEADS)).reshape(1, DT_PAD)
    head_of_channel = jnp.arange(D_SSM, dtype=jnp.int32) // SSM_HEAD_DIM
    expand = (jnp.arange(DT_PAD, dtype=jnp.int32)[:, None] == head_of_channel[None, :]).astype(MXU_DTYPE)
    return {
        "cwa": jnp.pad(conv_dw_w[i].astype(f32), ((0, A_TAIL - CONV_A_WIDTH), (0, 0))),
        "cba": row(conv_dw_b[i]), "lng": row(conv_ln_g[i]), "lnb": row(conv_ln_b[i]),
        "cwb": jnp.pad(ssm_conv_w[i].astype(f32), ((0, B_TAIL - SSM_CONV_WIDTH), (0, 0))),
        "cbb": row(ssm_conv_b[i]),
        "dtb": pad_heads(ssm_dt_bias[i]),
        "a": pad_heads(-jnp.exp(ssm_a_log[i].astype(f32))),
        "dexp": jnp.repeat(ssm_d[i].astype(f32), SSM_HEAD_DIM).reshape(1, D_SSM),
        "nrm": row(ssm_norm[i]),
        "e": expand,
    }


def kernel(x_prompt, x_sample, state_conv_a, state_conv_b, state_ssm, ffn1_norm, ffn1_w_gate, ffn1_w_up, ffn1_w_down, mix_norm, w_in, conv_dw_w, conv_dw_b, conv_ln_g, conv_ln_b, ssm_conv_w, ssm_conv_b, ssm_dt_bias, ssm_a_log, ssm_d, ssm_norm, w_out, ffn2_norm, ffn2_w_gate, ffn2_w_up, ffn2_w_down, final_norm):
    depth = ffn1_norm.shape[0]
    bp, lp, _ = x_prompt.shape
    bs = x_sample.shape[0]
    f32 = jnp.float32
    row = lambda v: v.reshape(1, -1).astype(f32)
    w16 = lambda w: w.astype(MXU_DTYPE)

    yp = x_prompt.reshape(bp * lp, D_MODEL)
    ysm = x_sample.reshape(bs, D_MODEL)
    outs = [[] for _ in range(6)]
    for i in range(depth):
        p = _prep_params(i, conv_dw_w, conv_dw_b, conv_ln_g, conv_ln_b, ssm_conv_w, ssm_conv_b, ssm_dt_bias,
                         ssm_a_log, ssm_d, ssm_norm)
        w_in_t = jnp.transpose(w_in[i])
        n_main = 2 * D_CONV + D_SSM + D_XBC
        w_dt = jnp.pad(w_in_t[n_main:], ((0, DT_PAD - SSM_HEADS), (0, 0)))
        f1 = (row(ffn1_norm[i]), w16(ffn1_w_gate[i]), w16(ffn1_w_up[i]), w16(ffn1_w_down[i]))
        fin = row(final_norm) if i == depth - 1 else None

        x1, w2g, w2u, w2d, wo, w_main = _ffn(
            yp, *f1, cast=(ffn2_w_gate[i], ffn2_w_up[i], ffn2_w_down[i], w_out[i]), cast_t=((w_in_t, n_main),))
        f2 = (row(ffn2_norm[i]), w2g, w2u, w2d)
        ya, a_tail, z, xbc, dt_raw = _inproj_conv(x1, row(mix_norm[i]), w_main, w_dt,
                                                  (p["cwa"], p["cba"], p["lng"], p["lnb"]), lp)
        xbc3 = xbc.reshape(bp, lp, D_XBC)
        ys, ssm_t = _ssd(z.reshape(bp, lp, D_SSM), xbc3, dt_raw.reshape(bp, lp, DT_PAD), p)
        yp = _ffn(x1, *f2, ya=ya, ys=ys.reshape(bp * lp, D_SSM), wo=wo, final_g=fin)
        outs[0].append(a_tail[:, A_TAIL - (CONV_A_WIDTH - 1):, :].astype(x_prompt.dtype))
        outs[1].append(xbc3[:, lp - (SSM_CONV_WIDTH - 1):, :].astype(x_prompt.dtype))
        outs[2].append(ssm_t.reshape(bp, SSM_HEADS, SSM_HEAD_DIM, D_STATE).astype(state_ssm.dtype))

        s1 = _ffn(ysm, *f1)
        su, sz, sxbc, sdt = _inproj(s1, row(mix_norm[i]), w_main, w_dt)
        tap_major = lambda st: jnp.transpose(st.astype(f32), (1, 0, 2))
        new_a, sua = _sample_conv_a(tap_major(state_conv_a[i]), su, p)
        new_b, xs, xt, dat, b_mat, ct = _sample_conv_b(tap_major(state_conv_b[i]), sxbc, sdt, p)
        new_s, yt = _sample_ssd(state_ssm[i].astype(f32), xt, dat, b_mat, ct)
        sya, sys_ = _sample_gate(sua, yt, xs, sz, p)
        ysm = _ffn(s1, *f2, ya=sya, ys=sys_, wo=wo, final_g=fin)
        outs[3].append(jnp.transpose(new_a, (1, 0, 2)).astype(x_sample.dtype))
        outs[4].append(jnp.transpose(new_b, (1, 0, 2)).astype(x_sample.dtype))
        outs[5].append(new_s.astype(state_ssm.dtype))

    return (yp.reshape(bp, lp, D_MODEL), ysm.reshape(bs, 1, D_MODEL)) + tuple(jnp.stack(o) for o in outs)
```
